```python
import math
import jax
import jax.numpy as jnp
from jax import lax
import numpy as np

D_MODEL = 1024
BATCH = 4
SEQ = 4096
DEPTH = 4

N_A_LAYERS = DEPTH // 2
N_B_LAYERS = DEPTH - N_A_LAYERS
N_DENSE = (DEPTH + 1) // 2
N_MOE = DEPTH // 2

CONV_K = 31

N_HEADS = 16
HEAD_DIM = D_MODEL // N_HEADS
MOBA_BLOCK = 256
MOBA_TOPK = 3
Q_CHUNK = 32

D_FF = ((8 * D_MODEL // 3 + 255) // 256) * 256
N_EXPERTS = 8
TOP_K_EXPERTS = 2
D_FF_EXPERT = 7 * D_MODEL // 2

ALPHA = (2.0 * DEPTH) ** 0.25
BETA = (8.0 * DEPTH) ** -0.25

LN_EPS = 1e-5
NEG_INF = -1e30
ADA_INIT = 0.5

kernel_name = "yoco_conformer_moba_deepnorm_moe"


def layer_norm(x, g, b):
    xf = x.astype(jnp.float32)
    mu = jnp.mean(xf, axis=-1, keepdims=True)
    xc = xf - mu
    var = jnp.mean(xc * xc, axis=-1, keepdims=True)
    y = xc * lax.rsqrt(var + LN_EPS)
    return (y * g.astype(jnp.float32) + b.astype(jnp.float32)).astype(x.dtype)


def alibi_slopes(n_heads):
    return jnp.exp2(-8.0 * jnp.arange(1, n_heads + 1, dtype=jnp.float32) / n_heads)


def swiglu(h, w13, w2):
    a, b = jnp.split(h @ w13, 2, axis=-1)
    return (jax.nn.silu(a) * b) @ w2


def conv_module(h, w_in, b_in, w_dw, b_dw, g_n, b_n, w_out, b_out):
    a, g = jnp.split(h @ w_in + b_in, 2, axis=-1)
    u = a * jax.nn.sigmoid(g)
    u = lax.conv_general_dilated(
        u, w_dw[:, None, :], window_strides=(1,), padding=[(CONV_K - 1, 0)],
        dimension_numbers=("NWC", "WIO", "NWC"),
        feature_group_count=u.shape[-1]) + b_dw
    u = jax.nn.silu(layer_norm(u, g_n, b_n))
    return u @ w_out + b_out


def moe_swiglu(h, router_w, router_b, w13, w2):
    logits = (h @ router_w).astype(jnp.float32) + router_b.astype(jnp.float32)
    top_val, top_idx = lax.top_k(logits, TOP_K_EXPERTS)
    top_w = jax.nn.softmax(top_val, axis=-1)
    gates = jnp.sum(jax.nn.one_hot(top_idx, N_EXPERTS, dtype=jnp.float32) * top_w[..., None], axis=-2)
    gates = gates.astype(h.dtype)
    y = jnp.zeros_like(h)
    for e in range(N_EXPERTS):
        y = y + gates[..., e:e + 1] * swiglu(h, w13[e], w2[e])
    return y


def shared_kv(h, w_kv):
    B, S, _ = h.shape
    n_blk = -(-S // MOBA_BLOCK)
    pad = n_blk * MOBA_BLOCK - S
    k, v = jnp.split(h @ w_kv, 2, axis=-1)
    k = k.reshape(B, S, N_HEADS, HEAD_DIM).transpose(0, 2, 1, 3)
    v = v.reshape(B, S, N_HEADS, HEAD_DIM).transpose(0, 2, 1, 3)
    k_pad = jnp.pad(k, ((0, 0), (0, 0), (0, pad), (0, 0)))
    v_pad = jnp.pad(v, ((0, 0), (0, 0), (0, pad), (0, 0)))
    k_mean = jnp.mean(k_pad.reshape(B, N_HEADS, n_blk, MOBA_BLOCK, HEAD_DIM).astype(jnp.float32), axis=3)
    return k_pad, v_pad, k_mean


def moba_attention(q, k_pad, v_pad, k_mean, slopes):
    B, S, H, Dh = q.shape
    n_blk = k_mean.shape[2]
    topk = min(MOBA_TOPK, n_blk)
    kb = k_pad.reshape(B, H, n_blk, MOBA_BLOCK, Dh)
    vb = v_pad.reshape(B, H, n_blk, MOBA_BLOCK, Dh)
    n_chunk = S // Q_CHUNK
    q_chunks = q.transpose(0, 2, 1, 3).reshape(B, H, n_chunk, Q_CHUNK, Dh).transpose(2, 0, 1, 3, 4)
    scale = Dh ** -0.5
    b_idx = jnp.arange(B)[:, None, None, None]
    h_idx = jnp.arange(H)[None, :, None, None]
    blk_pos = jnp.arange(MOBA_BLOCK)

    def chunk(args):
        ci, qc = args
        start = ci * Q_CHUNK
        blk = start // MOBA_BLOCK
        t = start + jnp.arange(Q_CHUNK)
        gate = jnp.einsum("bhqd,bhnd->bhqn", qc.astype(jnp.float32), k_mean)
        gate = jnp.where(jnp.arange(n_blk) < blk, gate, -jnp.inf)
        _, idx = lax.top_k(gate, topk)
        valid = idx < blk
        k_sel = kb[b_idx, h_idx, idx]
        v_sel = vb[b_idx, h_idx, idx]
        dist_sel = (t[:, None, None] - (idx[..., None] * MOBA_BLOCK + blk_pos)).astype(jnp.float32)
        s_sel = jnp.einsum("bhqd,bhqnkd->bhqnk", qc, k_sel).astype(jnp.float32) * scale
        s_sel = s_sel - slopes[:, None, None, None] * dist_sel
        s_sel = jnp.where(valid[..., None], s_sel, NEG_INF)
        k_own = lax.dynamic_slice_in_dim(k_pad, blk * MOBA_BLOCK, MOBA_BLOCK, axis=2)
        v_own = lax.dynamic_slice_in_dim(v_pad, blk * MOBA_BLOCK, MOBA_BLOCK, axis=2)
        dist_own = t[:, None] - (blk * MOBA_BLOCK + blk_pos)[None, :]
        s_own = jnp.einsum("bhqd,bhkd->bhqk", qc, k_own).astype(jnp.float32) * scale
        s_own = s_own - slopes[:, None, None] * dist_own.astype(jnp.float32)
        s_own = jnp.where(dist_own >= 0, s_own, NEG_INF)
        scores = jnp.concatenate([s_sel.reshape(B, H, Q_CHUNK, topk * MOBA_BLOCK), s_own], axis=-1)
        p = jax.nn.softmax(scores, axis=-1).astype(v_pad.dtype)
        p_sel = p[..., :topk * MOBA_BLOCK].reshape(B, H, Q_CHUNK, topk, MOBA_BLOCK)
        p_own = p[..., topk * MOBA_BLOCK:]
        return (jnp.einsum("bhqnk,bhqnkd->bhqd", p_sel, v_sel)
                + jnp.einsum("bhqk,bhkd->bhqd", p_own, v_own))

    outs = lax.map(chunk, (jnp.arange(n_chunk), q_chunks))
    return outs.transpose(1, 0, 3, 2, 4).reshape(B, S, H * Dh)


def setup_inputs(seed: int = 0) -> dict:
    key = jax.random.key(seed)
    ks = jax.random.split(key, 32)
    D = D_MODEL

    def nrm(k, shape, std):
        return jax.random.normal(k, shape, dtype=jnp.float32) * std

    return {
        "x": nrm(ks[0], (BATCH, SEQ, D), 1.0),
        "c": nrm(ks[1], (BATCH, D), 1.0),
        "ada_w": nrm(ks[2], (DEPTH, D, 6 * D), ADA_INIT * D ** -0.5),
        "ada_b": nrm(ks[3], (DEPTH, 6 * D), 0.01),
        "ln_g": 1.0 + nrm(ks[4], (DEPTH, 2, D), 0.05),
        "ln_b": nrm(ks[5], (DEPTH, 2, D), 0.01),
        "conv_in_w": nrm(ks[6], (N_A_LAYERS, D, 2 * D), D ** -0.5),
        "conv_in_b": nrm(ks[7], (N_A_LAYERS, 2 * D), 0.01),
        "conv_dw_w": nrm(ks[8], (N_A_LAYERS, CONV_K, D), CONV_K ** -0.5),
        "conv_dw_b": nrm(ks[9], (N_A_LAYERS, D), 0.01),
        "conv_norm_g": 1.0 + nrm(ks[10], (N_A_LAYERS, D), 0.05),
        "conv_norm_b": nrm(ks[11], (N_A_LAYERS, D), 0.01),
        "conv_out_w": nrm(ks[12], (N_A_LAYERS, D, D), BETA * D ** -0.5),
        "conv_out_b": nrm(ks[13], (N_A_LAYERS, D), 0.01),
        "kv_ada_w": nrm(ks[14], (D, 2 * D), ADA_INIT * D ** -0.5),
        "kv_ada_b": nrm(ks[15], (2 * D,), 0.01),
        "w_kv": nrm(ks[16], (D, 2 * D), D ** -0.5),
        "w_q": nrm(ks[17], (N_B_LAYERS, D, D), D ** -0.5),
        "w_o": nrm(ks[18], (N_B_LAYERS, D, D), BETA * D ** -0.5),
        "ffn_w13": nrm(ks[19], (N_DENSE, D, 2 * D_FF), D ** -0.5),
        "ffn_w2": nrm(ks[20], (N_DENSE, D_FF, D), BETA * D_FF ** -0.5),
        "router_w": nrm(ks[21], (N_MOE, D, N_EXPERTS), D ** -0.5),
        "router_b": nrm(ks[22], (N_MOE, N_EXPERTS), 0.01),
        "moe_w13": nrm(ks[23], (N_MOE, N_EXPERTS, D, 2 * D_FF_EXPERT), D ** -0.5),
        "moe_w2": nrm(ks[24], (N_MOE, N_EXPERTS, D_FF_EXPERT, D), BETA * D_FF_EXPERT ** -0.5),
    }


def reference(x, c, ada_w, ada_b, ln_g, ln_b, conv_in_w, conv_in_b, conv_dw_w, conv_dw_b,
              conv_norm_g, conv_norm_b, conv_out_w, conv_out_b, kv_ada_w, kv_ada_b, w_kv,
              w_q, w_o, ffn_w13, ffn_w2, router_w, router_b, moe_w13, moe_w2):
    B, S, D = x.shape
    slopes = alibi_slopes(N_HEADS)
    c_act = jax.nn.silu(c)
    k_pad = v_pad = k_mean = None
    for l in range(DEPTH):
        if l == N_A_LAYERS:
            kv_shift, kv_scale = jnp.split(c_act @ kv_ada_w + kv_ada_b, 2, axis=-1)
            h_kv = x * (1.0 + kv_scale[:, None, :]) + kv_shift[:, None, :]
            k_pad, v_pad, k_mean = shared_kv(h_kv, w_kv)
        sh1, sc1, g1, sh2, sc2, g2 = jnp.split(c_act @ ada_w[l] + ada_b[l], 6, axis=-1)
        h = x * (1.0 + sc1[:, None, :]) + sh1[:, None, :]
        if l < N_A_LAYERS:
            out = conv_module(h, conv_in_w[l], conv_in_b[l], conv_dw_w[l], conv_dw_b[l],
                              conv_norm_g[l], conv_norm_b[l], conv_out_w[l], conv_out_b[l])
        else:
            j = l - N_A_LAYERS
            q = (h @ w_q[j]).reshape(B, S, N_HEADS, HEAD_DIM)
            out = moba_attention(q, k_pad, v_pad, k_mean, slopes) @ w_o[j]
        x = layer_norm(ALPHA * x + (1.0 + g1[:, None, :]) * out, ln_g[l, 0], ln_b[l, 0])
        h = x * (1.0 + sc2[:, None, :]) + sh2[:, None, :]
        if l % 2 == 0:
            f = swiglu(h, ffn_w13[l // 2], ffn_w2[l // 2])
        else:
            f = moe_swiglu(h, router_w[l // 2], router_b[l // 2], moe_w13[l // 2], moe_w2[l // 2])
        x = layer_norm(ALPHA * x + (1.0 + g2[:, None, :]) * f, ln_g[l, 1], ln_b[l, 1])
    return x
```

```python
import functools

import jax
import jax.numpy as jnp
from jax import lax
from jax.experimental import pallas as pl
from jax.experimental.pallas import tpu as pltpu

F32 = jnp.float32
BF16 = jnp.bfloat16
HIGHEST = lax.Precision.HIGHEST

D_MODEL = 1024
DEPTH = 4
N_A_LAYERS = DEPTH // 2
CONV_K = 31
N_HEADS = 16
HEAD_DIM = D_MODEL // N_HEADS
MOBA_BLOCK = 256
MOBA_TOPK = 3
D_FF = 2816
N_EXPERTS = 8
D_FF_EXPERT = 3584
ALPHA = (2.0 * DEPTH) ** 0.25
LN_EPS = 1e-5
NEG_INF = -1e30

SUBLANES = 8
LANES = 128
VMEM_LIMIT = 48 * 1024 * 1024

ROW_TILE = 512
CONV_HALO = 32
CONV_TILE = 256
CONV_CHUNK = 32
FFN_CHUNK = 1408
MOE_CHUNK = 896
MOE_TILE = 512
HEAD_PAIR = 2
AUG = 2 * HEAD_DIM


def _params(sem, vmem=VMEM_LIMIT):
    return pltpu.CompilerParams(dimension_semantics=sem, vmem_limit_bytes=vmem)


def _layer_norm(y, g, b):
    mu = jnp.mean(y, axis=-1, keepdims=True)
    yc = y - mu
    var = jnp.mean(yc * yc, axis=-1, keepdims=True)
    return yc * lax.rsqrt(var + LN_EPS) * g + b


def _modulate(x, sc, sh):
    return x * (1.0 + sc) + sh


def _ada_kernel(c_ref, w_ref, b_ref, o_ref):
    ca = jax.nn.silu(c_ref[...])
    o_ref[...] = jnp.dot(ca, w_ref[...], precision=HIGHEST, preferred_element_type=F32) + b_ref[...]


def _ada(c_pad, w, b, tn):
    n_l, d, n = w.shape
    rows = c_pad.shape[0]
    return pl.pallas_call(
        _ada_kernel,
        grid=(n_l, n // tn),
        in_specs=[
            pl.BlockSpec((rows, d), lambda l, j: (0, 0)),
            pl.BlockSpec((None, d, tn), lambda l, j: (l, 0, j)),
            pl.BlockSpec((None, 1, tn), lambda l, j: (l, 0, j)),
        ],
        out_specs=pl.BlockSpec((None, rows, tn), lambda l, j: (l, 0, j)),
        out_shape=jax.ShapeDtypeStruct((n_l, rows, n), F32),
        compiler_params=_params(("arbitrary", "arbitrary")),
        name="ada",
    )(c_pad, w, b.reshape(n_l, 1, n))


def _proj_kernel(x_ref, sc_ref, sh_ref, w_ref, b_ref, o_ref, *, glu, out_scale):
    h = _modulate(x_ref[...], sc_ref[...], sh_ref[...]).astype(BF16)
    y = jnp.dot(h, w_ref[...], preferred_element_type=F32) + b_ref[...]
    if glu:
        half = y.shape[1] // 2
        y = y[:, :half] * jax.nn.sigmoid(y[:, half:])
    if out_scale != 1.0:
        y = y * out_scale
    o_ref[...] = y.astype(o_ref.dtype)


def _proj(x, sc, sh, w, b, seq, *, glu, out_scale, out_dtype, name):
    t, d = x.shape
    n = w.shape[1]
    n_out = n // 2 if glu else n
    tm = ROW_TILE
    per_seq = seq // tm
    return pl.pallas_call(
        functools.partial(_proj_kernel, glu=glu, out_scale=out_scale),
        grid=(t // tm,),
        in_specs=[
            pl.BlockSpec((tm, d), lambda i: (i, 0)),
            pl.BlockSpec((None, 1, d), lambda i: (i // per_seq, 0, 0)),
            pl.BlockSpec((None, 1, d), lambda i: (i // per_seq, 0, 0)),
            pl.BlockSpec((d, n), lambda i: (0, 0)),
            pl.BlockSpec((1, n), lambda i: (0, 0)),
        ],
        out_specs=pl.BlockSpec((tm, n_out), lambda i: (i, 0)),
        out_shape=jax.ShapeDtypeStruct((t, n_out), out_dtype),
        compiler_params=_params(("arbitrary",)),
        name=name,
    )(x, sc, sh, w, b)


def _conv_kernel(uprev_ref, u_ref, x_ref, g1_ref, wdw_ref, bdw_ref, gn_ref, bn_ref, wo_ref, bo_ref,
                 lng_ref, lnb_ref, o_ref, win_ref, cv_ref, *, per_seq):
    tm = u_ref.shape[0]
    first = (pl.program_id(0) % per_seq) == 0
    win_ref[0:CONV_HALO, :] = jnp.where(first, 0.0, uprev_ref[...])
    win_ref[CONV_HALO:, :] = u_ref[...]
    off = CONV_HALO - (CONV_K - 1)
    for c in range(tm // CONV_CHUNK):
        base = c * CONV_CHUNK
        acc = jnp.zeros((CONV_CHUNK, D_MODEL), F32) + bdw_ref[...]
        for k in range(CONV_K):
            acc = acc + win_ref[base + off + k:base + off + k + CONV_CHUNK, :] * wdw_ref[k:k + 1, :]
        cv_ref[base:base + CONV_CHUNK, :] = acc
    un = jax.nn.silu(_layer_norm(cv_ref[...], gn_ref[...], bn_ref[...])).astype(BF16)
    out = jnp.dot(un, wo_ref[...], preferred_element_type=F32) + bo_ref[...]
    y = ALPHA * x_ref[...] + (1.0 + g1_ref[...]) * out
    o_ref[...] = _layer_norm(y, lng_ref[...], lnb_ref[...])


def _conv_block(u, x, g1, wdw, bdw, gn, bn, wo, bo, lng, lnb, seq):
    t, d = x.shape
    tm = CONV_TILE
    per_seq = seq // tm
    halo_per_tile = tm // CONV_HALO
    vec = lambda: pl.BlockSpec((1, d), lambda i: (0, 0))
    return pl.pallas_call(
        functools.partial(_conv_kernel, per_seq=per_seq),
        grid=(t // tm,),
        in_specs=[
            pl.BlockSpec((CONV_HALO, d), lambda i: (jnp.maximum(i * halo_per_tile - 1, 0), 0)),
            pl.BlockSpec((tm, d), lambda i: (i, 0)),
            pl.BlockSpec((tm, d), lambda i: (i, 0)),
            pl.BlockSpec((None, 1, d), lambda i: (i // per_seq, 0, 0)),
            pl.BlockSpec((CONV_K, d), lambda i: (0, 0)),
            vec(), vec(), vec(),
            pl.BlockSpec((d, d), lambda i: (0, 0)),
            vec(), vec(), vec(),
        ],
        out_specs=pl.BlockSpec((tm, d), lambda i: (i, 0)),
        out_shape=jax.ShapeDtypeStruct((t, d), F32),
        scratch_shapes=[pltpu.VMEM((CONV_HALO + tm, d), F32), pltpu.VMEM((tm, d), F32)],
        compiler_params=_params(("arbitrary",)),
        name="conv_block",
    )(u, u, x, g1, wdw, bdw, gn, bn, wo, bo, lng, lnb)


def _oproj_kernel(a_ref, x_ref, g1_ref, w_ref, lng_ref, lnb_ref, o_ref):
    out = jnp.dot(a_ref[...], w_ref[...], preferred_element_type=F32)
    y = ALPHA * x_ref[...] + (1.0 + g1_ref[...]) * out
    o_ref[...] = _layer_norm(y, lng_ref[...], lnb_ref[...])


def _oproj(a, x, g1, w, lng, lnb, seq):
    t, d = x.shape
    tm = ROW_TILE
    per_seq = seq // tm
    vec = lambda: pl.BlockSpec((1, d), lambda i: (0, 0))
    return pl.pallas_call(
        _oproj_kernel,
        grid=(t // tm,),
        in_specs=[
            pl.BlockSpec((tm, d), lambda i: (i, 0)),
            pl.BlockSpec((tm, d), lambda i: (i, 0)),
            pl.BlockSpec((None, 1, d), lambda i: (i // per_seq, 0, 0)),
            pl.BlockSpec((d, d), lambda i: (0, 0)),
            vec(), vec(),
        ],
        out_specs=pl.BlockSpec((tm, d), lambda i: (i, 0)),
        out_shape=jax.ShapeDtypeStruct((t, d), F32),
        compiler_params=_params(("arbitrary",)),
        name="oproj",
    )(a, x, g1, w, lng, lnb)


def _swiglu_chunk(hb, wa_ref, wb_ref, w2_ref):
    a = jnp.dot(hb, wa_ref[...], preferred_element_type=F32)
    b = jnp.dot(hb, wb_ref[...], preferred_element_type=F32)
    g = (jax.nn.silu(a) * b).astype(BF16)
    return jnp.dot(g, w2_ref[...], preferred_element_type=F32)


def _ffn_kernel(x_ref, sc_ref, sh_ref, g2_ref, wa_ref, wb_ref, w2_ref, lng_ref, lnb_ref, o_ref,
                hb_ref, acc_ref):
    j = pl.program_id(1)

    @pl.when(j == 0)
    def _():
        hb_ref[...] = _modulate(x_ref[...], sc_ref[...], sh_ref[...]).astype(BF16)
        acc_ref[...] = jnp.zeros_like(acc_ref)

    acc_ref[...] += _swiglu_chunk(hb_ref[...], wa_ref, wb_ref, w2_ref)

    @pl.when(j == pl.num_programs(1) - 1)
    def _():
        y = ALPHA * x_ref[...] + (1.0 + g2_ref[...]) * acc_ref[...]
        o_ref[...] = _layer_norm(y, lng_ref[...], lnb_ref[...])


def _ffn(x, sc, sh, g2, w13, w2, lng, lnb, seq):
    t, d = x.shape
    f = w2.shape[0]
    tm, fc = ROW_TILE, FFN_CHUNK
    n_fc = f // fc
    per_seq = seq // tm
    mod = lambda: pl.BlockSpec((None, 1, d), lambda i, j: (i // per_seq, 0, 0))
    vec = lambda: pl.BlockSpec((1, d), lambda i, j: (0, 0))
    return pl.pallas_call(
        _ffn_kernel,
        grid=(t // tm, n_fc),
        in_specs=[
            pl.BlockSpec((tm, d), lambda i, j: (i, 0)),
            mod(), mod(), mod(),
            pl.BlockSpec((d, fc), lambda i, j: (0, j)),
            pl.BlockSpec((d, fc), lambda i, j: (0, j + n_fc)),
            pl.BlockSpec((fc, d), lambda i, j: (j, 0)),
            vec(), vec(),
        ],
        out_specs=pl.BlockSpec((tm, d), lambda i, j: (i, 0)),
        out_shape=jax.ShapeDtypeStruct((t, d), F32),
        scratch_shapes=[pltpu.VMEM((tm, d), BF16), pltpu.VMEM((tm, d), F32)],
        compiler_params=_params(("arbitrary", "arbitrary")),
        name="ffn",
    )(x, sc, sh, g2, w13, w13, w2, lng, lnb)


def _router_kernel(x_ref, sc_ref, sh_ref, rwt_ref, rb_ref, h_ref, rows_ref, cols_ref):
    h = _modulate(x_ref[...], sc_ref[...], sh_ref[...])
    h_ref[...] = h
    lg = lax.dot_general(rwt_ref[...], h, (((1,), (1,)), ((), ())), precision=HIGHEST,
                         preferred_element_type=F32) + rb_ref[...]
    idx = lax.broadcasted_iota(jnp.int32, lg.shape, 0)
    m1 = jnp.max(lg, axis=0, keepdims=True)
    i1 = jnp.min(jnp.where(lg == m1, idx, N_EXPERTS), axis=0, keepdims=True)
    lg2 = jnp.where(idx == i1, -jnp.inf, lg)
    m2 = jnp.max(lg2, axis=0, keepdims=True)
    i2 = jnp.min(jnp.where(lg2 == m2, idx, N_EXPERTS), axis=0, keepdims=True)
    e2 = jnp.exp(m2 - m1)
    den = 1.0 + e2
    w1 = 1.0 / den
    w2 = e2 / den
    meta = jnp.where(idx == 0, i1.astype(F32),
                     jnp.where(idx == 1, i2.astype(F32),
                               jnp.where(idx == 2, w1, jnp.where(idx == 3, w2, 0.0))))
    rows_ref[...] = meta
    wide = jnp.concatenate([meta, jnp.zeros((LANES - N_EXPERTS, meta.shape[1]), F32)], axis=0)
    cols_ref[...] = wide.T


def _router(x, sc, sh, rwt, rb, seq):
    t, d = x.shape
    tm = ROW_TILE
    per_seq = seq // tm
    mod = lambda: pl.BlockSpec((None, 1, d), lambda i: (i // per_seq, 0, 0))
    return pl.pallas_call(
        _router_kernel,
        grid=(t // tm,),
        in_specs=[
            pl.BlockSpec((tm, d), lambda i: (i, 0)),
            mod(), mod(),
            pl.BlockSpec((N_EXPERTS, d), lambda i: (0, 0)),
            pl.BlockSpec((N_EXPERTS, 1), lambda i: (0, 0)),
        ],
        out_specs=[
            pl.BlockSpec((tm, d), lambda i: (i, 0)),
            pl.BlockSpec((N_EXPERTS, tm), lambda i: (0, i)),
            pl.BlockSpec((tm, LANES), lambda i: (i, 0)),
        ],
        out_shape=[
            jax.ShapeDtypeStruct((t, d), F32),
            jax.ShapeDtypeStruct((N_EXPERTS, t), F32),
            jax.ShapeDtypeStruct((t, LANES), F32),
        ],
        compiler_params=_params(("arbitrary",)),
        name="router",
    )(x, sc, sh, rwt, rb)


def _moe_kernel(te_ref, nv_ref, dcur_ref, dnext_ref, h_hbm, wa_ref, wb_ref, w2_ref, y_hbm,
                hbuf, hb_ref, acc_ref, obuf, gsem, ssem, *, n_tok):
    i = pl.program_id(0)
    j = pl.program_id(1)
    n_tiles = pl.num_programs(0)
    n_fc = pl.num_programs(1)
    tm = hbuf.shape[0]

    def gather_copy(tok, r):
        return pltpu.make_async_copy(h_hbm.at[pl.ds(tok, 1), :], hbuf.at[pl.ds(r, 1), :], gsem.at[0])

    def scatter_copy(r, dst):
        return pltpu.make_async_copy(obuf.at[pl.ds(r, 1), :], y_hbm.at[pl.ds(dst, 1), :], ssem.at[0])

    def start_gather(d_ref):
        def body(r, carry):
            gather_copy(d_ref[0, r] & (n_tok - 1), r).start()
            return carry
        lax.fori_loop(0, tm, body, 0)

    def wait_gather():
        def body(r, carry):
            gather_copy(0, 0).wait()
            return carry
        lax.fori_loop(0, tm, body, 0)

    def wait_scatter(n_rows):
        def body(r, carry):
            scatter_copy(0, 0).wait()
            return carry
        lax.fori_loop(0, n_rows, body, 0)

    nv = nv_ref[i]

    @pl.when(j == 0)
    def _():
        @pl.when(i == 0)
        def _():
            start_gather(dcur_ref)

        wait_gather()
        hb_ref[...] = hbuf[...].astype(BF16)

        @pl.when(i + 1 < n_tiles)
        def _():
            start_gather(dnext_ref)

    @pl.when(nv > 0)
    def _():
        contrib = _swiglu_chunk(hb_ref[...], wa_ref, wb_ref, w2_ref)

        @pl.when(j == 0)
        def _():
            acc_ref[...] = contrib

        @pl.when(j > 0)
        def _():
            acc_ref[...] += contrib

    @pl.when(j == n_fc - 1)
    def _():
        @pl.when(i > 0)
        def _():
            wait_scatter(nv_ref[jnp.maximum(i - 1, 0)])

        obuf[...] = acc_ref[...]

        def body(r, carry):
            scatter_copy(r, dcur_ref[0, r]).start()
            return carry
        lax.fori_loop(0, nv, body, 0)

        @pl.when(i == n_tiles - 1)
        def _():
            wait_scatter(nv)


def _moe(h, tile_expert, tile_nvalid, row_dst, w13, w2):
    t, d = h.shape
    n_tiles, _, tm = row_dst.shape
    f = w2.shape[1]
    fc = MOE_CHUNK
    n_fc = f // fc
    grid_spec = pltpu.PrefetchScalarGridSpec(
        num_scalar_prefetch=2,
        grid=(n_tiles, n_fc),
        in_specs=[
            pl.BlockSpec((None, 1, tm), lambda i, j, te, nv: (i, 0, 0), memory_space=pltpu.SMEM),
            pl.BlockSpec((None, 1, tm), lambda i, j, te, nv: (jnp.minimum(i + 1, n_tiles - 1), 0, 0),
                         memory_space=pltpu.SMEM),
            pl.BlockSpec(memory_space=pl.ANY),
            pl.BlockSpec((None, d, fc), lambda i, j, te, nv: (te[i], 0, j)),
            pl.BlockSpec((None, d, fc), lambda i, j, te, nv: (te[i], 0, j + n_fc)),
            pl.BlockSpec((None, fc, d), lambda i, j, te, nv: (te[i], j, 0)),
        ],
        out_specs=pl.BlockSpec(memory_space=pl.ANY),
        scratch_shapes=[
            pltpu.VMEM((tm, d), F32),
            pltpu.VMEM((tm, d), BF16),
            pltpu.VMEM((tm, d), F32),
            pltpu.VMEM((tm, d), F32),
            pltpu.SemaphoreType.DMA((1,)),
            pltpu.SemaphoreType.DMA((1,)),
        ],
    )
    return pl.pallas_call(
        functools.partial(_moe_kernel, n_tok=t),
        grid_spec=grid_spec,
        out_shape=jax.ShapeDtypeStruct((2 * t, d), F32),
        compiler_params=_params(("arbitrary", "arbitrary")),
        name="moe",
    )(tile_expert, tile_nvalid, row_dst, row_dst, h, w13, w13, w2)


def _route_rows(e1, e2, tm, n_tiles):
    t = e1.shape[0]
    experts = jnp.concatenate([e1, e2])
    order = jnp.argsort(experts, stable=True).astype(jnp.int32)
    counts = jnp.sum(experts[:, None] == jnp.arange(N_EXPERTS)[None, :], axis=0).astype(jnp.int32)
    tiles_per = (counts + tm - 1) // tm
    tile_end = jnp.cumsum(tiles_per)
    tile_start = tile_end - tiles_per
    group_start = jnp.cumsum(counts) - counts
    tile_ids = jnp.arange(n_tiles, dtype=jnp.int32)
    te = jnp.minimum(jnp.sum(tile_ids[:, None] >= tile_end[None, :], axis=1), N_EXPERTS - 1).astype(jnp.int32)
    used = tile_ids < tile_end[-1]
    first_row = (tile_ids - tile_start[te]) * tm
    nvalid = jnp.where(used, jnp.clip(counts[te] - first_row, 0, tm), 0).astype(jnp.int32)
    last_e = te[jnp.maximum(tile_end[-1] - 1, 0)]
    te = jnp.where(used, te, last_e)
    r = jnp.arange(tm, dtype=jnp.int32)
    src = group_start[te][:, None] + first_row[:, None] + r[None, :]
    valid = r[None, :] < nvalid[:, None]
    row_dst = jnp.where(valid, order[jnp.clip(src, 0, 2 * t - 1)], -1).astype(jnp.int32)
    return te, nvalid, row_dst.reshape(n_tiles, 1, tm)


def _combine_kernel(x_ref, y0_ref, y1_ref, gw_ref, g2_ref, lng_ref, lnb_ref, o_ref):
    gw = gw_ref[...]
    f = gw[:, 2:3] * y0_ref[...] + gw[:, 3:4] * y1_ref[...]
    y = ALPHA * x_ref[...] + (1.0 + g2_ref[...]) * f
    o_ref[...] = _layer_norm(y, lng_ref[...], lnb_ref[...])


def _combine(x, y, gw, g2, lng, lnb, seq):
    t, d = x.shape
    tm = ROW_TILE
    per_seq = seq // tm
    n_t = t // tm
    vec = lambda: pl.BlockSpec((1, d), lambda i: (0, 0))
    return pl.pallas_call(
        _combine_kernel,
        grid=(n_t,),
        in_specs=[
            pl.BlockSpec((tm, d), lambda i: (i, 0)),
            pl.BlockSpec((tm, d), lambda i: (i, 0)),
            pl.BlockSpec((tm, d), lambda i: (i + n_t, 0)),
            pl.BlockSpec((tm, LANES), lambda i: (i, 0)),
            pl.BlockSpec((None, 1, d), lambda i: (i // per_seq, 0, 0)),
            vec(), vec(),
        ],
        out_specs=pl.BlockSpec((tm, d), lambda i: (i, 0)),
        out_shape=jax.ShapeDtypeStruct((t, d), F32),
        compiler_params=_params(("arbitrary",)),
        name="combine",
    )(x, y, y, gw, g2, lng, lnb)


def _kv_kernel(x_ref, sc_ref, sh_ref, w_ref, ka_ref, va_ref, km_ref, *, blocks_per_seq):
    h = _modulate(x_ref[...], sc_ref[...], sh_ref[...]).astype(BF16)
    kv = jnp.dot(h, w_ref[...], preferred_element_type=F32)
    k = kv[:, :D_MODEL]
    v = kv[:, D_MODEL:]
    km_ref[...] = jnp.mean(k, axis=0, keepdims=True)
    j = pl.program_id(0) % blocks_per_seq
    shape = (MOBA_BLOCK, HEAD_DIM)
    col = lax.broadcasted_iota(jnp.int32, shape, 1)
    pos = lax.broadcasted_iota(jnp.int32, shape, 0).astype(F32)
    blk = (j * MOBA_BLOCK).astype(F32)
    kx = jnp.where(col < 3, pos,
                   jnp.where(col < 6, blk,
                             jnp.where((col >= 8) & (col < 11), 1.0,
                                       jnp.where(col == 16 + j, 1.0, 0.0)))).astype(BF16)
    vx = jnp.where(col == 0, 1.0, 0.0).astype(BF16)
    for hd in range(N_HEADS):
        lo = hd * HEAD_DIM
        ka_ref[hd] = jnp.concatenate([k[:, lo:lo + HEAD_DIM].astype(BF16), kx], axis=1)
        va_ref[hd] = jnp.concatenate([v[:, lo:lo + HEAD_DIM].astype(BF16), vx], axis=1)


def _shared_kv(x, sc, sh, w, batch, seq):
    t, d = x.shape
    tm = MOBA_BLOCK
    n_blk = seq // tm
    mod = lambda: pl.BlockSpec((None, 1, d), lambda i: (i // n_blk, 0, 0))
    aug = lambda: pl.BlockSpec((None, N_HEADS, tm, AUG), lambda i: (i // n_blk, 0, i % n_blk, 0))
    return pl.pallas_call(
        functools.partial(_kv_kernel, blocks_per_seq=n_blk),
        grid=(t // tm,),
        in_specs=[
            pl.BlockSpec((tm, d), lambda i: (i, 0)),
            mod(), mod(),
            pl.BlockSpec((d, 2 * d), lambda i: (0, 0)),
        ],
        out_specs=[aug(), aug(), pl.BlockSpec((None, None, 1, d), lambda i: (i // n_blk, i % n_blk, 0, 0))],
        out_shape=[
            jax.ShapeDtypeStruct((batch, N_HEADS, seq, AUG), BF16),
            jax.ShapeDtypeStruct((batch, N_HEADS, seq, AUG), BF16),
            jax.ShapeDtypeStruct((batch, n_blk, 1, d), F32),
        ],
        compiler_params=_params(("arbitrary",)),
        name="shared_kv",
    )(x, sc, sh, w)


def _split3(x):
    p1 = x.astype(BF16).astype(F32)
    p2 = (x - p1).astype(BF16).astype(F32)
    p3 = (x - p1 - p2).astype(BF16).astype(F32)
    return p1, p2, p3


def _dot_nt(a, b):
    return lax.dot_general(a, b, (((1,), (1,)), ((), ())), preferred_element_type=F32)


def _attn_kernel(sl_ref, q_ref, ka_ref, va_ref, km_ref, o_ref, *, n_blk):
    hp = pl.program_id(1)
    i = pl.program_id(2)
    bq = MOBA_BLOCK
    lane_q = lax.broadcasted_iota(jnp.int32, (SUBLANES, bq), 1).astype(F32)
    sub8 = lax.broadcasted_iota(jnp.int32, (SUBLANES, bq), 0)
    blk_id = lax.broadcasted_iota(jnp.int32, (n_blk, bq), 0)
    row = lax.broadcasted_iota(jnp.int32, (bq, bq), 0)
    colk = lax.broadcasted_iota(jnp.int32, (bq, bq), 1)
    q_all = q_ref[...]
    outs = []
    for hh in range(HEAD_PAIR):
        lo = hh * HEAD_DIM
        qh = q_all[:, lo:lo + HEAD_DIM]
        gate = lax.dot_general(km_ref[:, lo:lo + HEAD_DIM], qh.astype(F32), (((1,), (1,)), ((), ())),
                               precision=HIGHEST, preferred_element_type=F32)
        gate = jnp.where(blk_id < i, gate, -jnp.inf)
        rank = jnp.zeros((n_blk, bq), jnp.int32)
        for m in range(n_blk):
            gm = gate[m:m + 1, :]
            ahead = (gm > gate) | ((gm == gate) & (blk_id > m))
            rank = rank + ahead.astype(jnp.int32)
        bias = jnp.where((blk_id < i) & (rank >= MOBA_TOPK), NEG_INF, 0.0)
        slope = jnp.full((SUBLANES, bq), sl_ref[hp * HEAD_PAIR + hh], F32)
        s1, s2, s3 = _split3(slope)
        t1, t2, t3 = _split3(-slope * (lane_q + (i * bq).astype(F32)))
        rows_s = jnp.where((sub8 == 0) | (sub8 == 3), s1,
                           jnp.where((sub8 == 1) | (sub8 == 4), s2,
                                     jnp.where((sub8 == 2) | (sub8 == 5), s3, 0.0)))
        rows_t = jnp.where(sub8 == 0, t1, jnp.where(sub8 == 1, t2, jnp.where(sub8 == 2, t3, 0.0)))
        ext_t = jnp.concatenate([rows_s, rows_t, bias, jnp.zeros((LANES - 2 * SUBLANES - n_blk, bq), F32)],
                                axis=0)
        q_aug = jnp.concatenate([qh, ext_t.T[:, :HEAD_DIM].astype(BF16)], axis=1)

        own = pl.multiple_of(i * bq, bq)
        s_own = _dot_nt(q_aug, ka_ref[hh, pl.ds(own, bq), :])
        s_own = jnp.where(colk <= row, s_own, NEG_INF)
        m0 = jnp.max(s_own, axis=1, keepdims=True)
        p0 = jnp.exp(s_own - m0).astype(BF16)
        acc0 = jnp.dot(p0, va_ref[hh, pl.ds(own, bq), :], preferred_element_type=F32)

        def body(jb, carry):
            m_run, acc = carry
            start = pl.multiple_of(jb * bq, bq)
            s = _dot_nt(q_aug, ka_ref[hh, pl.ds(start, bq), :])
            m_new = jnp.maximum(m_run, jnp.max(s, axis=1, keepdims=True))
            p = jnp.exp(s - m_new).astype(BF16)
            acc = jnp.exp(m_run - m_new) * acc + jnp.dot(p, va_ref[hh, pl.ds(start, bq), :],
                                                         preferred_element_type=F32)
            return m_new, acc

        _, acc = lax.fori_loop(0, i, body, (m0, acc0))
        outs.append(acc[:, :HEAD_DIM] / acc[:, HEAD_DIM:HEAD_DIM + 1])
    o_ref[...] = jnp.concatenate(outs, axis=1).astype(o_ref.dtype)


def _attention(slopes, q, ka, va, km, batch, seq):
    n_blk = seq // MOBA_BLOCK
    d = q.shape[-1]
    grid_spec = pltpu.PrefetchScalarGridSpec(
        num_scalar_prefetch=1,
        grid=(batch, N_HEADS // HEAD_PAIR, n_blk),
        in_specs=[
            pl.BlockSpec((None, MOBA_BLOCK, AUG), lambda b, hp, i, sl: (b, i, hp)),
            pl.BlockSpec((None, HEAD_PAIR, seq, AUG), lambda b, hp, i, sl: (b, hp, 0, 0)),
            pl.BlockSpec((None, HEAD_PAIR, seq, AUG), lambda b, hp, i, sl: (b, hp, 0, 0)),
            pl.BlockSpec((None, n_blk, AUG), lambda b, hp, i, sl: (b, 0, hp)),
        ],
        out_specs=pl.BlockSpec((None, MOBA_BLOCK, AUG), lambda b, hp, i, sl: (b, i, hp)),
    )
    return pl.pallas_call(
        functools.partial(_attn_kernel, n_blk=n_blk),
        grid_spec=grid_spec,
        out_shape=jax.ShapeDtypeStruct((batch, seq, d), BF16),
        compiler_params=_params(("arbitrary", "arbitrary", "arbitrary")),
        name="moba_attn",
    )(slopes, q, ka, va, km)


def kernel(x, c, ada_w, ada_b, ln_g, ln_b, conv_in_w, conv_in_b, conv_dw_w, conv_dw_b, conv_norm_g,
           conv_norm_b, conv_out_w, conv_out_b, kv_ada_w, kv_ada_b, w_kv, w_q, w_o, ffn_w13, ffn_w2,
           router_w, router_b, moe_w13, moe_w2):
    batch, seq, d = x.shape
    t = batch * seq
    assert d == D_MODEL and seq % ROW_TILE == 0 and seq % MOBA_BLOCK == 0 and t & (t - 1) == 0
    n_blk = seq // MOBA_BLOCK
    assert n_blk <= 16

    c_pad = jnp.pad(c, ((0, SUBLANES - batch), (0, 0)))
    mods = _ada(c_pad, ada_w, ada_b, 1536)[:, :batch]
    kv_mod = _ada(c_pad, kv_ada_w[None], kv_ada_b[None], 1024)[0, :batch]

    def mod_vec(v):
        return v.reshape(batch, 1, d)

    row = lambda v: v.reshape(1, -1)
    slopes = jnp.exp2(-8.0 * jnp.arange(1, N_HEADS + 1, dtype=F32) / N_HEADS)
    zero_bias = jnp.zeros((1, d), F32)
    n_moe_tiles = 2 * t // MOE_TILE + N_EXPERTS

    xf = x.reshape(t, d)
    ka = va = km = None
    for l in range(DEPTH):
        if l == N_A_LAYERS:
            ka, va, km = _shared_kv(xf, mod_vec(kv_mod[:, d:]), mod_vec(kv_mod[:, :d]),
                                    w_kv.astype(BF16), batch, seq)
            km = km.reshape(batch, n_blk, d)
        sh1, sc1, g1, sh2, sc2, g2 = [mod_vec(mods[l, :, k * d:(k + 1) * d]) for k in range(6)]
        lng1, lnb1 = row(ln_g[l, 0]), row(ln_b[l, 0])
        lng2, lnb2 = row(ln_g[l, 1]), row(ln_b[l, 1])
        if l < N_A_LAYERS:
            u = _proj(xf, sc1, sh1, conv_in_w[l].astype(BF16), row(conv_in_b[l]), seq, glu=True,
                      out_scale=1.0, out_dtype=F32, name="conv_in")
            xf = _conv_block(u, xf, g1, conv_dw_w[l], row(conv_dw_b[l]), row(conv_norm_g[l]),
                             row(conv_norm_b[l]), conv_out_w[l].astype(BF16), row(conv_out_b[l]),
                             lng1, lnb1, seq)
        else:
            jl = l - N_A_LAYERS
            q = _proj(xf, sc1, sh1, w_q[jl].astype(BF16), zero_bias, seq, glu=False,
                      out_scale=HEAD_DIM ** -0.5, out_dtype=BF16, name="q_proj")
            att = _attention(slopes, q.reshape(batch, seq, d), ka, va, km, batch, seq)
            xf = _oproj(att.reshape(t, d), xf, g1, w_o[jl].astype(BF16), lng1, lnb1, seq)
        if l % 2 == 0:
            xf = _ffn(xf, sc2, sh2, g2, ffn_w13[l // 2].astype(BF16), ffn_w2[l // 2].astype(BF16),
                      lng2, lnb2, seq)
        else:
            e = l // 2
            h, meta_rows, meta_cols = _router(xf, sc2, sh2, router_w[e].T, router_b[e].reshape(-1, 1), seq)
            te, nvalid, row_dst = _route_rows(meta_rows[0].astype(jnp.int32), meta_rows[1].astype(jnp.int32),
                                              MOE_TILE, n_moe_tiles)
            y = _moe(h, te, nvalid, row_dst, moe_w13[e].astype(BF16), moe_w2[e].astype(BF16))
            xf = _combine(xf, y, meta_cols, g2, lng2, lnb2, seq)
    return xf.reshape(batch, seq, d)
```

```python
import functools

import jax
import jax.numpy as jnp
from jax import lax
from jax.experimental import pallas as pl
from jax.experimental.pallas import tpu as pltpu

F32 = jnp.float32
BF16 = jnp.bfloat16
HIGHEST = lax.Precision.HIGHEST

D_MODEL = 1024
DEPTH = 4
N_A_LAYERS = DEPTH // 2
CONV_K = 31
N_HEADS = 16
HEAD_DIM = D_MODEL // N_HEADS
MOBA_BLOCK = 256
MOBA_TOPK = 3
D_FF = 2816
N_EXPERTS = 8
D_FF_EXPERT = 3584
ALPHA = (2.0 * DEPTH) ** 0.25
LN_EPS = 1e-5
NEG_INF = -1e30

SUBLANES = 8
LANES = 128
VMEM_LIMIT = 48 * 1024 * 1024

ROW_TILE = 512
CONV_HALO = 32
CONV_TILE = 256
CONV_CHUNK = 32
FFN_CHUNK = 1408
MOE_CHUNK = 896
MOE_TILE = 512
HEAD_PAIR = 2
AUG = 2 * HEAD_DIM
EXT = 32
LOG2E = 1.4426950408889634


def _params(sem, vmem=VMEM_LIMIT):
    return pltpu.CompilerParams(dimension_semantics=sem, vmem_limit_bytes=vmem)


def _layer_norm(y, g, b):
    mu = jnp.mean(y, axis=-1, keepdims=True)
    yc = y - mu
    var = jnp.mean(yc * yc, axis=-1, keepdims=True)
    return yc * lax.rsqrt(var + LN_EPS) * g + b


def _modulate(x, sc, sh):
    return x * (1.0 + sc) + sh


def _ada_kernel(c_ref, w_ref, b_ref, o_ref):
    ca = jax.nn.silu(c_ref[...])
    o_ref[...] = jnp.dot(ca, w_ref[...], precision=HIGHEST, preferred_element_type=F32) + b_ref[...]


def _ada(c_pad, w, b, tn):
    n_l, d, n = w.shape
    rows = c_pad.shape[0]
    return pl.pallas_call(
        _ada_kernel,
        grid=(n_l, n // tn),
        in_specs=[
            pl.BlockSpec((rows, d), lambda l, j: (0, 0)),
            pl.BlockSpec((None, d, tn), lambda l, j: (l, 0, j)),
            pl.BlockSpec((None, 1, tn), lambda l, j: (l, 0, j)),
        ],
        out_specs=pl.BlockSpec((None, rows, tn), lambda l, j: (l, 0, j)),
        out_shape=jax.ShapeDtypeStruct((n_l, rows, n), F32),
        compiler_params=_params(("arbitrary", "arbitrary")),
        name="ada",
    )(c_pad, w, b.reshape(n_l, 1, n))


def _conv_in_kernel(x_ref, sc_ref, sh_ref, w_ref, b_ref, o_ref):
    h = _modulate(x_ref[...], sc_ref[...], sh_ref[...]).astype(BF16)
    y = jnp.dot(h, w_ref[...], preferred_element_type=F32) + b_ref[...]
    half = y.shape[1] // 2
    o_ref[...] = y[:, :half] * jax.nn.sigmoid(y[:, half:])


def _conv_in(x, sc, sh, w, b, seq):
    t, d = x.shape
    n = w.shape[1]
    n_out = n // 2
    tm = ROW_TILE
    per_seq = seq // tm
    return pl.pallas_call(
        _conv_in_kernel,
        grid=(t // tm,),
        in_specs=[
            pl.BlockSpec((tm, d), lambda i: (i, 0)),
            pl.BlockSpec((None, 1, d), lambda i: (i // per_seq, 0, 0)),
            pl.BlockSpec((None, 1, d), lambda i: (i // per_seq, 0, 0)),
            pl.BlockSpec((d, n), lambda i: (0, 0)),
            pl.BlockSpec((1, n), lambda i: (0, 0)),
        ],
        out_specs=pl.BlockSpec((tm, n_out), lambda i: (i, 0)),
        out_shape=jax.ShapeDtypeStruct((t, n_out), F32),
        compiler_params=_params(("arbitrary",)),
        name="conv_in",
    )(x, sc, sh, w, b)


def _conv_kernel(uprev_ref, u_ref, x_ref, g1_ref, wdw_ref, bdw_ref, gn_ref, bn_ref, wo_ref, bo_ref,
                 lng_ref, lnb_ref, o_ref, win_ref, cv_ref, *, per_seq):
    tm = u_ref.shape[0]
    first = (pl.program_id(0) % per_seq) == 0
    win_ref[0:CONV_HALO, :] = jnp.where(first, 0.0, uprev_ref[...])
    win_ref[CONV_HALO:, :] = u_ref[...]
    off = CONV_HALO - (CONV_K - 1)
    for c in range(tm // CONV_CHUNK):
        base = c * CONV_CHUNK
        acc = jnp.zeros((CONV_CHUNK, D_MODEL), F32) + bdw_ref[...]
        for k in range(CONV_K):
            acc = acc + win_ref[base + off + k:base + off + k + CONV_CHUNK, :] * wdw_ref[k:k + 1, :]
        cv_ref[base:base + CONV_CHUNK, :] = acc
    un = jax.nn.silu(_layer_norm(cv_ref[...], gn_ref[...], bn_ref[...])).astype(BF16)
    out = jnp.dot(un, wo_ref[...], preferred_element_type=F32) + bo_ref[...]
    y = ALPHA * x_ref[...] + (1.0 + g1_ref[...]) * out
    o_ref[...] = _layer_norm(y, lng_ref[...], lnb_ref[...])


def _conv_block(u, x, g1, wdw, bdw, gn, bn, wo, bo, lng, lnb, seq):
    t, d = x.shape
    tm = CONV_TILE
    per_seq = seq // tm
    halo_per_tile = tm // CONV_HALO
    vec = lambda: pl.BlockSpec((1, d), lambda i: (0, 0))
    return pl.pallas_call(
        functools.partial(_conv_kernel, per_seq=per_seq),
        grid=(t // tm,),
        in_specs=[
            pl.BlockSpec((CONV_HALO, d), lambda i: (jnp.maximum(i * halo_per_tile - 1, 0), 0)),
            pl.BlockSpec((tm, d), lambda i: (i, 0)),
            pl.BlockSpec((tm, d), lambda i: (i, 0)),
            pl.BlockSpec((None, 1, d), lambda i: (i // per_seq, 0, 0)),
            pl.BlockSpec((CONV_K, d), lambda i: (0, 0)),
            vec(), vec(), vec(),
            pl.BlockSpec((d, d), lambda i: (0, 0)),
            vec(), vec(), vec(),
        ],
        out_specs=pl.BlockSpec((tm, d), lambda i: (i, 0)),
        out_shape=jax.ShapeDtypeStruct((t, d), F32),
        scratch_shapes=[pltpu.VMEM((CONV_HALO + tm, d), F32), pltpu.VMEM((tm, d), F32)],
        compiler_params=_params(("arbitrary",)),
        name="conv_block",
    )(u, u, x, g1, wdw, bdw, gn, bn, wo, bo, lng, lnb)


def _oproj_kernel(a_ref, x_ref, g1_ref, w_ref, lng_ref, lnb_ref, o_ref):
    out = jnp.dot(a_ref[...], w_ref[...], preferred_element_type=F32)
    y = ALPHA * x_ref[...] + (1.0 + g1_ref[...]) * out
    o_ref[...] = _layer_norm(y, lng_ref[...], lnb_ref[...])


def _oproj(a, x, g1, w, lng, lnb, seq):
    t, d = x.shape
    tm = ROW_TILE
    per_seq = seq // tm
    vec = lambda: pl.BlockSpec((1, d), lambda i: (0, 0))
    return pl.pallas_call(
        _oproj_kernel,
        grid=(t // tm,),
        in_specs=[
            pl.BlockSpec((tm, d), lambda i: (i, 0)),
            pl.BlockSpec((tm, d), lambda i: (i, 0)),
            pl.BlockSpec((None, 1, d), lambda i: (i // per_seq, 0, 0)),
            pl.BlockSpec((d, d), lambda i: (0, 0)),
            vec(), vec(),
        ],
        out_specs=pl.BlockSpec((tm, d), lambda i: (i, 0)),
        out_shape=jax.ShapeDtypeStruct((t, d), F32),
        compiler_params=_params(("arbitrary",)),
        name="oproj",
    )(a, x, g1, w, lng, lnb)


def _swiglu_chunk(hb, wa_ref, wb_ref, w2_ref):
    a = jnp.dot(hb, wa_ref[...], preferred_element_type=F32)
    b = jnp.dot(hb, wb_ref[...], preferred_element_type=F32)
    g = (jax.nn.silu(a) * b).astype(BF16)
    return jnp.dot(g, w2_ref[...], preferred_element_type=F32)


def _ffn_kernel(x_ref, sc_ref, sh_ref, g2_ref, wa_ref, wb_ref, w2_ref, lng_ref, lnb_ref, o_ref,
                hb_ref, acc_ref):
    j = pl.program_id(1)

    @pl.when(j == 0)
    def _():
        hb_ref[...] = _modulate(x_ref[...], sc_ref[...], sh_ref[...]).astype(BF16)
        acc_ref[...] = jnp.zeros_like(acc_ref)

    acc_ref[...] += _swiglu_chunk(hb_ref[...], wa_ref, wb_ref, w2_ref)

    @pl.when(j == pl.num_programs(1) - 1)
    def _():
        y = ALPHA * x_ref[...] + (1.0 + g2_ref[...]) * acc_ref[...]
        o_ref[...] = _layer_norm(y, lng_ref[...], lnb_ref[...])


def _ffn(x, sc, sh, g2, w13, w2, lng, lnb, seq):
    t, d = x.shape
    f = w2.shape[0]
    tm, fc = ROW_TILE, FFN_CHUNK
    n_fc = f // fc
    per_seq = seq // tm
    mod = lambda: pl.BlockSpec((None, 1, d), lambda i, j: (i // per_seq, 0, 0))
    vec = lambda: pl.BlockSpec((1, d), lambda i, j: (0, 0))
    return pl.pallas_call(
        _ffn_kernel,
        grid=(t // tm, n_fc),
        in_specs=[
            pl.BlockSpec((tm, d), lambda i, j: (i, 0)),
            mod(), mod(), mod(),
            pl.BlockSpec((d, fc), lambda i, j: (0, j)),
            pl.BlockSpec((d, fc), lambda i, j: (0, j + n_fc)),
            pl.BlockSpec((fc, d), lambda i, j: (j, 0)),
            vec(), vec(),
        ],
        out_specs=pl.BlockSpec((tm, d), lambda i, j: (i, 0)),
        out_shape=jax.ShapeDtypeStruct((t, d), F32),
        scratch_shapes=[pltpu.VMEM((tm, d), BF16), pltpu.VMEM((tm, d), F32)],
        compiler_params=_params(("arbitrary", "arbitrary")),
        name="ffn",
    )(x, sc, sh, g2, w13, w13, w2, lng, lnb)


def _router_kernel(x_ref, sc_ref, sh_ref, rwt_ref, rb_ref, h_ref, rows_ref, cols_ref):
    h = _modulate(x_ref[...], sc_ref[...], sh_ref[...])
    h_ref[...] = h
    lg = lax.dot_general(rwt_ref[...], h, (((1,), (1,)), ((), ())), precision=HIGHEST,
                         preferred_element_type=F32) + rb_ref[...]
    idx = lax.broadcasted_iota(jnp.int32, lg.shape, 0)
    m1 = jnp.max(lg, axis=0, keepdims=True)
    i1 = jnp.min(jnp.where(lg == m1, idx, N_EXPERTS), axis=0, keepdims=True)
    lg2 = jnp.where(idx == i1, -jnp.inf, lg)
    m2 = jnp.max(lg2, axis=0, keepdims=True)
    i2 = jnp.min(jnp.where(lg2 == m2, idx, N_EXPERTS), axis=0, keepdims=True)
    e2 = jnp.exp(m2 - m1)
    den = 1.0 + e2
    w1 = 1.0 / den
    w2 = e2 / den
    meta = jnp.where(idx == 0, i1.astype(F32),
                     jnp.where(idx == 1, i2.astype(F32),
                               jnp.where(idx == 2, w1, jnp.where(idx == 3, w2, 0.0))))
    rows_ref[...] = meta
    wide = jnp.concatenate([meta, jnp.zeros((LANES - N_EXPERTS, meta.shape[1]), F32)], axis=0)
    cols_ref[...] = wide.T


def _router(x, sc, sh, rwt, rb, seq):
    t, d = x.shape
    tm = ROW_TILE
    per_seq = seq // tm
    mod = lambda: pl.BlockSpec((None, 1, d), lambda i: (i // per_seq, 0, 0))
    return pl.pallas_call(
        _router_kernel,
        grid=(t // tm,),
        in_specs=[
            pl.BlockSpec((tm, d), lambda i: (i, 0)),
            mod(), mod(),
            pl.BlockSpec((N_EXPERTS, d), lambda i: (0, 0)),
            pl.BlockSpec((N_EXPERTS, 1), lambda i: (0, 0)),
        ],
        out_specs=[
            pl.BlockSpec((tm, d), lambda i: (i, 0)),
            pl.BlockSpec((N_EXPERTS, tm), lambda i: (0, i)),
            pl.BlockSpec((tm, LANES), lambda i: (i, 0)),
        ],
        out_shape=[
            jax.ShapeDtypeStruct((t, d), F32),
            jax.ShapeDtypeStruct((N_EXPERTS, t), F32),
            jax.ShapeDtypeStruct((t, LANES), F32),
        ],
        compiler_params=_params(("arbitrary",)),
        name="router",
    )(x, sc, sh, rwt, rb)


def _moe_kernel(te_ref, nv_ref, dcur_ref, dnext_ref, dprev_ref, h_hbm, wa_ref, wb_ref, w2_ref, y_hbm,
                hbuf, hb_ref, obuf, gsem, ssem, *, n_tok, n_fc):
    i = pl.program_id(0)
    j = pl.program_id(1)
    n_tiles = pl.num_programs(0)
    tm = hbuf.shape[0]
    rows_per_step = tm // n_fc
    slot = i % 2
    prev_slot = 1 - slot

    def gather_copy(tok, r):
        return pltpu.make_async_copy(h_hbm.at[pl.ds(tok, 1), :], hbuf.at[pl.ds(r, 1), :], gsem.at[0])

    def scatter_copy(r, dst):
        return pltpu.make_async_copy(obuf.at[prev_slot, pl.ds(r, 1), :], y_hbm.at[pl.ds(dst, 1), :],
                                     ssem.at[0])

    def wait_gather():
        pltpu.make_async_copy(hbuf, hbuf, gsem.at[0]).wait()

    def wait_scatter():
        pltpu.make_async_copy(obuf.at[0], obuf.at[0], ssem.at[0]).wait()

    @pl.when(j == 0)
    def _():
        @pl.when(i == 0)
        def _():
            obuf[...] = jnp.zeros_like(obuf)

            def body(r, carry):
                gather_copy(dcur_ref[0, r] & (n_tok - 1), r).start()
                return carry
            lax.fori_loop(0, tm, body, 0)

        wait_gather()
        hb_ref[...] = hbuf[...].astype(BF16)

        @pl.when(i >= 1)
        def _():
            wait_scatter()

    def issue_row_dmas():
        base = j * rows_per_step
        dump = 2 * n_tok + prev_slot * tm
        for r in range(rows_per_step):
            row = base + r
            gather_copy(dnext_ref[0, row] & (n_tok - 1), row).start()
            dst = dprev_ref[0, row]
            scatter_copy(row, jnp.where(dst < 0, dump + row, dst)).start()

    used = nv_ref[i] > 0

    @pl.when(used)
    def _():
        issue_row_dmas()
        contrib = _swiglu_chunk(hb_ref[...], wa_ref, wb_ref, w2_ref)

        @pl.when(j == 0)
        def _():
            obuf[slot] = contrib

        @pl.when(j > 0)
        def _():
            obuf[slot] += contrib

    @pl.when(jnp.logical_not(used))
    def _():
        issue_row_dmas()

    @pl.when((j == n_fc - 1) & (i == n_tiles - 1))
    def _():
        wait_gather()
        wait_scatter()


def _moe(h, tile_expert, tile_nvalid, row_dst, w13, w2):
    t, d = h.shape
    n_tiles, _, tm = row_dst.shape
    f = w2.shape[1]
    fc = MOE_CHUNK
    n_fc = f // fc
    prev_dst = jnp.concatenate([jnp.full((1, 1, tm), -1, jnp.int32), row_dst[:-1]], axis=0)
    grid_spec = pltpu.PrefetchScalarGridSpec(
        num_scalar_prefetch=2,
        grid=(n_tiles, n_fc),
        in_specs=[
            pl.BlockSpec((None, 1, tm), lambda i, j, te, nv: (i, 0, 0), memory_space=pltpu.SMEM),
            pl.BlockSpec((None, 1, tm), lambda i, j, te, nv: (jnp.minimum(i + 1, n_tiles - 1), 0, 0),
                         memory_space=pltpu.SMEM),
            pl.BlockSpec((None, 1, tm), lambda i, j, te, nv: (i, 0, 0), memory_space=pltpu.SMEM),
            pl.BlockSpec(memory_space=pl.ANY),
            pl.BlockSpec((None, d, fc), lambda i, j, te, nv: (te[i], 0, j)),
            pl.BlockSpec((None, d, fc), lambda i, j, te, nv: (te[i], 0, j + n_fc)),
            pl.BlockSpec((None, fc, d), lambda i, j, te, nv: (te[i], j, 0)),
        ],
        out_specs=pl.BlockSpec(memory_space=pl.ANY),
        scratch_shapes=[
            pltpu.VMEM((tm, d), F32),
            pltpu.VMEM((tm, d), BF16),
            pltpu.VMEM((2, tm, d), F32),
            pltpu.SemaphoreType.DMA((1,)),
            pltpu.SemaphoreType.DMA((1,)),
        ],
    )
    return pl.pallas_call(
        functools.partial(_moe_kernel, n_tok=t, n_fc=n_fc),
        grid_spec=grid_spec,
        out_shape=jax.ShapeDtypeStruct((2 * t + 2 * tm, d), F32),
        compiler_params=_params(("arbitrary", "arbitrary")),
        name="moe",
    )(tile_expert, tile_nvalid, row_dst, row_dst, prev_dst, h, w13, w13, w2)


def _route_rows(e1, e2, tm, n_tiles):
    t = e1.shape[0]
    experts = jnp.concatenate([e1, e2])
    order = jnp.argsort(experts, stable=True).astype(jnp.int32)
    counts = jnp.sum(experts[:, None] == jnp.arange(N_EXPERTS)[None, :], axis=0).astype(jnp.int32)
    tiles_per = (counts + tm - 1) // tm
    tile_end = jnp.cumsum(tiles_per)
    tile_start = tile_end - tiles_per
    group_start = jnp.cumsum(counts) - counts
    tile_ids = jnp.arange(n_tiles, dtype=jnp.int32)
    te = jnp.minimum(jnp.sum(tile_ids[:, None] >= tile_end[None, :], axis=1), N_EXPERTS - 1).astype(jnp.int32)
    used = tile_ids < tile_end[-1]
    first_row = (tile_ids - tile_start[te]) * tm
    nvalid = jnp.where(used, jnp.clip(counts[te] - first_row, 0, tm), 0).astype(jnp.int32)
    last_e = te[jnp.maximum(tile_end[-1] - 1, 0)]
    te = jnp.where(used, te, last_e)
    r = jnp.arange(tm, dtype=jnp.int32)
    src = group_start[te][:, None] + first_row[:, None] + r[None, :]
    valid = r[None, :] < nvalid[:, None]
    row_dst = jnp.where(valid, order[jnp.clip(src, 0, 2 * t - 1)], -1).astype(jnp.int32)
    return te, nvalid, row_dst.reshape(n_tiles, 1, tm)


def _combine_kernel(x_ref, y0_ref, y1_ref, gw_ref, g2_ref, lng_ref, lnb_ref, o_ref):
    gw = gw_ref[...]
    f = gw[:, 2:3] * y0_ref[...] + gw[:, 3:4] * y1_ref[...]
    y = ALPHA * x_ref[...] + (1.0 + g2_ref[...]) * f
    o_ref[...] = _layer_norm(y, lng_ref[...], lnb_ref[...])


def _combine(x, y, gw, g2, lng, lnb, seq):
    t, d = x.shape
    tm = ROW_TILE
    per_seq = seq // tm
    n_t = t // tm
    vec = lambda: pl.BlockSpec((1, d), lambda i: (0, 0))
    return pl.pallas_call(
        _combine_kernel,
        grid=(n_t,),
        in_specs=[
            pl.BlockSpec((tm, d), lambda i: (i, 0)),
            pl.BlockSpec((tm, d), lambda i: (i, 0)),
            pl.BlockSpec((tm, d), lambda i: (i + n_t, 0)),
            pl.BlockSpec((tm, LANES), lambda i: (i, 0)),
            pl.BlockSpec((None, 1, d), lambda i: (i // per_seq, 0, 0)),
            vec(), vec(),
        ],
        out_specs=pl.BlockSpec((tm, d), lambda i: (i, 0)),
        out_shape=jax.ShapeDtypeStruct((t, d), F32),
        compiler_params=_params(("arbitrary",)),
        name="combine",
    )(x, y, y, gw, g2, lng, lnb)


def _kv_kernel(x_ref, sc_ref, sh_ref, w_ref, ka_ref, va_ref, km_ref, *, blocks_per_seq):
    h = _modulate(x_ref[...], sc_ref[...], sh_ref[...]).astype(BF16)
    kv = jnp.dot(h, w_ref[...], preferred_element_type=F32)
    k = kv[:, :D_MODEL]
    v = kv[:, D_MODEL:]
    km_ref[...] = jnp.mean(k, axis=0, keepdims=True)
    j = pl.program_id(0) % blocks_per_seq
    shape = (MOBA_BLOCK, HEAD_DIM)
    col = lax.broadcasted_iota(jnp.int32, shape, 1)
    pos = lax.broadcasted_iota(jnp.int32, shape, 0).astype(F32)
    blk = (j * MOBA_BLOCK).astype(F32)
    kx = jnp.where(col < 3, pos,
                   jnp.where(col < 6, blk,
                             jnp.where((col >= 8) & (col < 11), 1.0,
                                       jnp.where(col == 16 + j, 1.0, 0.0)))).astype(BF16)
    vx = jnp.where(col == 0, 1.0, 0.0).astype(BF16)
    for hd in range(N_HEADS):
        lo = hd * HEAD_DIM
        ka_ref[hd] = jnp.concatenate([k[:, lo:lo + HEAD_DIM].astype(BF16), kx], axis=1)
        va_ref[hd] = jnp.concatenate([v[:, lo:lo + HEAD_DIM].astype(BF16), vx], axis=1)


def _shared_kv(x, sc, sh, w, batch, seq):
    t, d = x.shape
    tm = MOBA_BLOCK
    n_blk = seq // tm
    mod = lambda: pl.BlockSpec((None, 1, d), lambda i: (i // n_blk, 0, 0))
    aug = lambda: pl.BlockSpec((None, N_HEADS, tm, AUG), lambda i: (i // n_blk, 0, i % n_blk, 0))
    return pl.pallas_call(
        functools.partial(_kv_kernel, blocks_per_seq=n_blk),
        grid=(t // tm,),
        in_specs=[
            pl.BlockSpec((tm, d), lambda i: (i, 0)),
            mod(), mod(),
            pl.BlockSpec((d, 2 * d), lambda i: (0, 0)),
        ],
        out_specs=[aug(), aug(), pl.BlockSpec((None, None, 1, d), lambda i: (i // n_blk, i % n_blk, 0, 0))],
        out_shape=[
            jax.ShapeDtypeStruct((batch, N_HEADS, seq, AUG), BF16),
            jax.ShapeDtypeStruct((batch, N_HEADS, seq, AUG), BF16),
            jax.ShapeDtypeStruct((batch, n_blk, 1, d), F32),
        ],
        compiler_params=_params(("arbitrary",)),
        name="shared_kv",
    )(x, sc, sh, w)


def _split3(x):
    p1 = x.astype(BF16).astype(F32)
    p2 = (x - p1).astype(BF16).astype(F32)
    p3 = (x - p1 - p2).astype(BF16).astype(F32)
    return p1, p2, p3


def _dot_nt(a, b):
    return lax.dot_general(a, b, (((1,), (1,)), ((), ())), preferred_element_type=F32)


def _qaug_kernel(sl_ref, x_ref, sc_ref, sh_ref, w_ref, kmbd_ref, qa_ref, *, n_blk):
    i = pl.program_id(0) % n_blk
    bq = MOBA_BLOCK
    h = _modulate(x_ref[...], sc_ref[...], sh_ref[...]).astype(BF16)
    qb = (jnp.dot(h, w_ref[...], preferred_element_type=F32) * (HEAD_DIM ** -0.5 * LOG2E)).astype(BF16)
    gates = _dot_nt(kmbd_ref[0], qb) + _dot_nt(kmbd_ref[1], qb) + _dot_nt(kmbd_ref[2], qb)
    lane_q = lax.broadcasted_iota(jnp.int32, (SUBLANES, bq), 1).astype(F32)
    sub8 = lax.broadcasted_iota(jnp.int32, (SUBLANES, bq), 0)
    blk_id = lax.broadcasted_iota(jnp.int32, (n_blk, bq), 0)
    t_q = lane_q + (i * bq).astype(F32)
    group = LANES // EXT
    for g0 in range(0, N_HEADS, group):
        exts = []
        for hd in range(g0, g0 + group):
            gate = jnp.where(blk_id < i, gates[hd * n_blk:(hd + 1) * n_blk, :], -jnp.inf)
            rank = jnp.zeros((n_blk, bq), jnp.int32)
            for m in range(n_blk):
                gm = gate[m:m + 1, :]
                ahead = (gm > gate) | ((gm == gate) & (blk_id > m))
                rank = rank + ahead.astype(jnp.int32)
            bias = jnp.where((blk_id < i) & (rank >= MOBA_TOPK), NEG_INF, 0.0)
            slope = jnp.full((SUBLANES, bq), sl_ref[hd], F32) * LOG2E
            s1, s2, s3 = _split3(slope)
            t1, t2, t3 = _split3(-slope * t_q)
            rows_s = jnp.where((sub8 == 0) | (sub8 == 3), s1,
                               jnp.where((sub8 == 1) | (sub8 == 4), s2,
                                         jnp.where((sub8 == 2) | (sub8 == 5), s3, 0.0)))
            rows_t = jnp.where(sub8 == 0, t1, jnp.where(sub8 == 1, t2, jnp.where(sub8 == 2, t3, 0.0)))
            pieces = [rows_s, rows_t, bias]
            if EXT - 2 * SUBLANES - n_blk:
                pieces.append(jnp.zeros((EXT - 2 * SUBLANES - n_blk, bq), F32))
            exts.extend(pieces)
        ext_t = jnp.concatenate(exts, axis=0).T.astype(BF16)
        pad = jnp.zeros((bq, AUG - HEAD_DIM - EXT), BF16)
        for k, hd in enumerate(range(g0, g0 + group)):
            qa_ref[hd] = jnp.concatenate([qb[:, hd * HEAD_DIM:(hd + 1) * HEAD_DIM],
                                          ext_t[:, k * EXT:(k + 1) * EXT], pad], axis=1)


def _q_aug(slopes, x, sc, sh, w, kmbd, batch, seq):
    t, d = x.shape
    tm = MOBA_BLOCK
    n_blk = seq // tm
    rows = kmbd.shape[2]
    mod = lambda: pl.BlockSpec((None, 1, d), lambda i, sl: (i // n_blk, 0, 0))
    grid_spec = pltpu.PrefetchScalarGridSpec(
        num_scalar_prefetch=1,
        grid=(t // tm,),
        in_specs=[
            pl.BlockSpec((tm, d), lambda i, sl: (i, 0)),
            mod(), mod(),
            pl.BlockSpec((d, d), lambda i, sl: (0, 0)),
            pl.BlockSpec((None, 3, rows, d), lambda i, sl: (i // n_blk, 0, 0, 0)),
        ],
        out_specs=pl.BlockSpec((None, N_HEADS, tm, AUG), lambda i, sl: (i // n_blk, 0, i % n_blk, 0)),
    )
    return pl.pallas_call(
        functools.partial(_qaug_kernel, n_blk=n_blk),
        grid_spec=grid_spec,
        out_shape=jax.ShapeDtypeStruct((batch, N_HEADS, seq, AUG), BF16),
        compiler_params=_params(("arbitrary",)),
        name="q_aug",
    )(slopes, x, sc, sh, w, kmbd)


def _block_diag_means(km, n_blk):
    d = km.shape[-1]
    r = jnp.arange(N_HEADS * n_blk)
    head_of_col = jnp.arange(d) // HEAD_DIM
    bd = jnp.where((r // n_blk)[None, :, None] == head_of_col[None, None, :], km[:, r % n_blk, :], 0.0)
    return jnp.stack(_split3(bd), axis=1).astype(BF16)


def _attn_kernel(qa_ref, ka_ref, va_ref, o_ref, s_scr, mx_scr, acc_scr, *, n_blk):
    i = pl.program_id(2)
    bq = MOBA_BLOCK
    null_slot = n_blk
    first = (pl.program_id(0) == 0) & (pl.program_id(1) == 0) & (i == 0)

    @pl.when(first)
    def _():
        for hh in range(HEAD_PAIR):
            s_scr[hh, null_slot] = jnp.full((bq, bq), NEG_INF, F32)

    for hh in range(HEAD_PAIR):
        mx_scr[hh] = jnp.full((bq, LANES), NEG_INF, F32)
        acc_scr[hh] = jnp.zeros((bq, AUG), F32)
    dmat = lax.broadcasted_iota(jnp.int32, (bq, bq), 1) - lax.broadcasted_iota(jnp.int32, (bq, bq), 0)
    n_pairs = (i + 2) // 2

    def scores(jj, carry):
        for blk in range(2):
            j = jnp.minimum(2 * jj + blk, i)
            start = pl.multiple_of(j * bq, bq)
            causal = dmat <= (i - j) * bq
            for hh in range(HEAD_PAIR):
                s = _dot_nt(qa_ref[hh], ka_ref[hh, pl.ds(start, bq), :])
                s = jnp.where(causal, s, NEG_INF)
                s_scr[hh, j] = s
                mx_scr[hh] = jnp.maximum(mx_scr[hh], jnp.maximum(s[:, :LANES], s[:, LANES:]))
        return carry

    lax.fori_loop(0, n_pairs, scores, 0)

    for hh in range(HEAD_PAIR):
        m = jnp.max(mx_scr[hh], axis=1, keepdims=True)
        mx_scr[hh] = jnp.broadcast_to(m, (bq, LANES))

    def values(jj, carry):
        for hh in range(HEAD_PAIR):
            m2 = jnp.concatenate([mx_scr[hh], mx_scr[hh]], axis=1)
            contrib = None
            for blk in range(2):
                j = 2 * jj + blk
                slot = jnp.where(j <= i, j, null_slot)
                start = pl.multiple_of(jnp.minimum(j, i) * bq, bq)
                p = jnp.exp2(s_scr[hh, slot] - m2).astype(BF16)
                pv = jnp.dot(p, va_ref[hh, pl.ds(start, bq), :], preferred_element_type=F32)
                contrib = pv if contrib is None else contrib + pv
            acc_scr[hh] += contrib
        return carry

    lax.fori_loop(0, n_pairs, values, 0)

    outs = []
    for hh in range(HEAD_PAIR):
        acc = acc_scr[hh]
        outs.append(acc[:, :HEAD_DIM] / acc[:, HEAD_DIM:HEAD_DIM + 1])
    o_ref[...] = jnp.concatenate(outs, axis=1).astype(o_ref.dtype)


def _attention(qa, ka, va, batch, seq):
    n_blk = seq // MOBA_BLOCK
    pair = lambda: pl.BlockSpec((None, HEAD_PAIR, seq, AUG), lambda b, hp, i: (b, hp, 0, 0))
    return pl.pallas_call(
        functools.partial(_attn_kernel, n_blk=n_blk),
        grid=(batch, N_HEADS // HEAD_PAIR, n_blk),
        in_specs=[
            pl.BlockSpec((None, HEAD_PAIR, MOBA_BLOCK, AUG), lambda b, hp, i: (b, hp, i, 0)),
            pair(), pair(),
        ],
        out_specs=pl.BlockSpec((None, MOBA_BLOCK, HEAD_PAIR * HEAD_DIM), lambda b, hp, i: (b, i, hp)),
        out_shape=jax.ShapeDtypeStruct((batch, seq, D_MODEL), BF16),
        scratch_shapes=[
            pltpu.VMEM((HEAD_PAIR, n_blk + 1, MOBA_BLOCK, MOBA_BLOCK), F32),
            pltpu.VMEM((HEAD_PAIR, MOBA_BLOCK, LANES), F32),
            pltpu.VMEM((HEAD_PAIR, MOBA_BLOCK, AUG), F32),
        ],
        compiler_params=_params(("arbitrary", "arbitrary", "arbitrary")),
        name="moba_attn",
    )(qa, ka, va)


def kernel(x, c, ada_w, ada_b, ln_g, ln_b, conv_in_w, conv_in_b, conv_dw_w, conv_dw_b, conv_norm_g,
           conv_norm_b, conv_out_w, conv_out_b, kv_ada_w, kv_ada_b, w_kv, w_q, w_o, ffn_w13, ffn_w2,
           router_w, router_b, moe_w13, moe_w2):
    batch, seq, d = x.shape
    t = batch * seq
    assert d == D_MODEL and seq % ROW_TILE == 0 and seq % MOBA_BLOCK == 0 and t & (t - 1) == 0
    n_blk = seq // MOBA_BLOCK
    assert n_blk <= 16

    c_pad = jnp.pad(c, ((0, SUBLANES - batch), (0, 0)))
    mods = _ada(c_pad, ada_w, ada_b, 1536)[:, :batch]
    kv_mod = _ada(c_pad, kv_ada_w[None], kv_ada_b[None], 1024)[0, :batch]

    def mod_vec(v):
        return v.reshape(batch, 1, d)

    row = lambda v: v.reshape(1, -1)
    slopes = jnp.exp2(-8.0 * jnp.arange(1, N_HEADS + 1, dtype=F32) / N_HEADS)
    n_moe_tiles = 2 * t // MOE_TILE + N_EXPERTS

    xf = x.reshape(t, d)
    ka = va = kmbd = None
    for l in range(DEPTH):
        if l == N_A_LAYERS:
            ka, va, km = _shared_kv(xf, mod_vec(kv_mod[:, d:]), mod_vec(kv_mod[:, :d]),
                                    w_kv.astype(BF16), batch, seq)
            kmbd = _block_diag_means(km.reshape(batch, n_blk, d), n_blk)
        sh1, sc1, g1, sh2, sc2, g2 = [mod_vec(mods[l, :, k * d:(k + 1) * d]) for k in range(6)]
        lng1, lnb1 = row(ln_g[l, 0]), row(ln_b[l, 0])
        lng2, lnb2 = row(ln_g[l, 1]), row(ln_b[l, 1])
        if l < N_A_LAYERS:
            u = _conv_in(xf, sc1, sh1, conv_in_w[l].astype(BF16), row(conv_in_b[l]), seq)
            xf = _conv_block(u, xf, g1, conv_dw_w[l], row(conv_dw_b[l]), row(conv_norm_g[l]),
                             row(conv_norm_b[l]), conv_out_w[l].astype(BF16), row(conv_out_b[l]),
                             lng1, lnb1, seq)
        else:
            jl = l - N_A_LAYERS
            qa = _q_aug(slopes, xf, sc1, sh1, w_q[jl].astype(BF16), kmbd, batch, seq)
            att = _attention(qa, ka, va, batch, seq)
            xf = _oproj(att.reshape(t, d), xf, g1, w_o[jl].astype(BF16), lng1, lnb1, seq)
        if l % 2 == 0:
            xf = _ffn(xf, sc2, sh2, g2, ffn_w13[l // 2].astype(BF16), ffn_w2[l // 2].astype(BF16),
                      lng2, lnb2, seq)
        else:
            e = l // 2
            h, meta_rows, meta_cols = _router(xf, sc2, sh2, router_w[e].T, router_b[e].reshape(-1, 1), seq)
            te, nvalid, row_dst = _route_rows(meta_rows[0].astype(jnp.int32), meta_rows[1].astype(jnp.int32),
                                              MOE_TILE, n_moe_tiles)
            y = _moe(h, te, nvalid, row_dst, moe_w13[e].astype(BF16), moe_w2[e].astype(BF16))
            xf = _combine(xf, y, meta_cols, g2, lng2, lnb2, seq)
    return xf.reshape(batch, seq, d)
```

```python
import functools

import jax
import jax.numpy as jnp
from jax import lax
from jax.experimental import pallas as pl
from jax.experimental.pallas import tpu as pltpu

F32 = jnp.float32
BF16 = jnp.bfloat16
HIGHEST = lax.Precision.HIGHEST

D_MODEL = 1024
DEPTH = 4
N_A_LAYERS = DEPTH // 2
CONV_K = 31
N_HEADS = 16
HEAD_DIM = D_MODEL // N_HEADS
MOBA_BLOCK = 256
MOBA_TOPK = 3
D_FF = 2816
N_EXPERTS = 8
D_FF_EXPERT = 3584
ALPHA = (2.0 * DEPTH) ** 0.25
LN_EPS = 1e-5
NEG_INF = -1e30

SUBLANES = 8
LANES = 128
VMEM_LIMIT = 48 * 1024 * 1024

ROW_TILE = 512
CONV_HALO = 32
CONV_TILE = 256
CONV_CHUNK = 32
FFN_CHUNK = 1408
MOE_CHUNK = 896
MOE_TILE = 512
HEAD_PAIR = 2
AUG = 2 * HEAD_DIM
EXT = 32
LOG2E = 1.4426950408889634


def _params(sem, vmem=VMEM_LIMIT):
    return pltpu.CompilerParams(dimension_semantics=sem, vmem_limit_bytes=vmem)


def _layer_norm(y, g, b):
    mu = jnp.mean(y, axis=-1, keepdims=True)
    yc = y - mu
    var = jnp.mean(yc * yc, axis=-1, keepdims=True)
    return yc * lax.rsqrt(var + LN_EPS) * g + b


def _modulate(x, sc, sh):
    return x * (1.0 + sc) + sh


def _ada_kernel(c_ref, w_ref, b_ref, o_ref):
    ca = jax.nn.silu(c_ref[...])
    o_ref[...] = jnp.dot(ca, w_ref[...], precision=HIGHEST, preferred_element_type=F32) + b_ref[...]


def _ada(c_pad, w, b, tn):
    n_l, d, n = w.shape
    rows = c_pad.shape[0]
    return pl.pallas_call(
        _ada_kernel,
        grid=(n_l, n // tn),
        in_specs=[
            pl.BlockSpec((rows, d), lambda l, j: (0, 0)),
            pl.BlockSpec((None, d, tn), lambda l, j: (l, 0, j)),
            pl.BlockSpec((None, 1, tn), lambda l, j: (l, 0, j)),
        ],
        out_specs=pl.BlockSpec((None, rows, tn), lambda l, j: (l, 0, j)),
        out_shape=jax.ShapeDtypeStruct((n_l, rows, n), F32),
        compiler_params=_params(("arbitrary", "arbitrary")),
        name="ada",
    )(c_pad, w, b.reshape(n_l, 1, n))


def _conv_in_kernel(x_ref, sc_ref, sh_ref, w_ref, b_ref, o_ref):
    h = _modulate(x_ref[...], sc_ref[...], sh_ref[...]).astype(BF16)
    y = jnp.dot(h, w_ref[...], preferred_element_type=F32) + b_ref[...]
    half = y.shape[1] // 2
    o_ref[...] = y[:, :half] * jax.nn.sigmoid(y[:, half:])


def _conv_in(x, sc, sh, w, layer, b, seq):
    t, d = x.shape
    n = w.shape[2]
    n_out = n // 2
    tm = ROW_TILE
    per_seq = seq // tm
    return pl.pallas_call(
        _conv_in_kernel,
        grid=(t // tm,),
        in_specs=[
            pl.BlockSpec((tm, d), lambda i: (i, 0)),
            pl.BlockSpec((None, 1, d), lambda i: (i // per_seq, 0, 0)),
            pl.BlockSpec((None, 1, d), lambda i: (i // per_seq, 0, 0)),
            pl.BlockSpec((None, d, n), lambda i: (layer, 0, 0)),
            pl.BlockSpec((1, n), lambda i: (0, 0)),
        ],
        out_specs=pl.BlockSpec((tm, n_out), lambda i: (i, 0)),
        out_shape=jax.ShapeDtypeStruct((t, n_out), F32),
        compiler_params=_params(("arbitrary",)),
        name="conv_in",
    )(x, sc, sh, w, b)


def _conv_kernel(uprev_ref, u_ref, x_ref, g1_ref, wdw_ref, bdw_ref, gn_ref, bn_ref, wo_ref, bo_ref,
                 lng_ref, lnb_ref, o_ref, win_ref, cv_ref, *, per_seq):
    tm = u_ref.shape[0]
    rows = CONV_HALO + tm
    first = (pl.program_id(0) % per_seq) == 0
    win_ref[0, 0:CONV_HALO, :] = jnp.where(first, 0.0, uprev_ref[...])
    win_ref[0, CONV_HALO:, :] = u_ref[...]
    for b in range(1, SUBLANES):
        win_ref[b, SUBLANES:, :] = win_ref[0, SUBLANES - b:rows - b, :]
    for c in range(tm // CONV_CHUNK):
        base = CONV_HALO + c * CONV_CHUNK
        acc = jnp.zeros((CONV_CHUNK, D_MODEL), F32) + bdw_ref[...]
        for s in range(CONV_K):
            a, b = divmod(s, SUBLANES)
            k = CONV_K - 1 - s
            acc = acc + win_ref[b, base - SUBLANES * a:base - SUBLANES * a + CONV_CHUNK, :] * wdw_ref[k:k + 1, :]
        cv_ref[c * CONV_CHUNK:(c + 1) * CONV_CHUNK, :] = acc
    un = jax.nn.silu(_layer_norm(cv_ref[...], gn_ref[...], bn_ref[...])).astype(BF16)
    out = jnp.dot(un, wo_ref[...], preferred_element_type=F32) + bo_ref[...]
    y = ALPHA * x_ref[...] + (1.0 + g1_ref[...]) * out
    o_ref[...] = _layer_norm(y, lng_ref[...], lnb_ref[...])


def _conv_block(u, x, g1, wdw, bdw, gn, bn, wo, layer, bo, lng, lnb, seq):
    t, d = x.shape
    tm = CONV_TILE
    per_seq = seq // tm
    halo_per_tile = tm // CONV_HALO
    vec = lambda: pl.BlockSpec((1, d), lambda i: (0, 0))
    return pl.pallas_call(
        functools.partial(_conv_kernel, per_seq=per_seq),
        grid=(t // tm,),
        in_specs=[
            pl.BlockSpec((CONV_HALO, d), lambda i: (jnp.maximum(i * halo_per_tile - 1, 0), 0)),
            pl.BlockSpec((tm, d), lambda i: (i, 0)),
            pl.BlockSpec((tm, d), lambda i: (i, 0)),
            pl.BlockSpec((None, 1, d), lambda i: (i // per_seq, 0, 0)),
            pl.BlockSpec((CONV_K, d), lambda i: (0, 0)),
            vec(), vec(), vec(),
            pl.BlockSpec((None, d, d), lambda i: (layer, 0, 0)),
            vec(), vec(), vec(),
        ],
        out_specs=pl.BlockSpec((tm, d), lambda i: (i, 0)),
        out_shape=jax.ShapeDtypeStruct((t, d), F32),
        scratch_shapes=[pltpu.VMEM((SUBLANES, CONV_HALO + tm, d), F32), pltpu.VMEM((tm, d), F32)],
        compiler_params=_params(("arbitrary",)),
        name="conv_block",
    )(u, u, x, g1, wdw, bdw, gn, bn, wo, bo, lng, lnb)


def _oproj_kernel(a_ref, x_ref, g1_ref, w_ref, lng_ref, lnb_ref, o_ref):
    out = jnp.dot(a_ref[...], w_ref[...], preferred_element_type=F32)
    y = ALPHA * x_ref[...] + (1.0 + g1_ref[...]) * out
    o_ref[...] = _layer_norm(y, lng_ref[...], lnb_ref[...])


def _oproj(a, x, g1, w, layer, lng, lnb, seq):
    t, d = x.shape
    tm = ROW_TILE
    per_seq = seq // tm
    vec = lambda: pl.BlockSpec((1, d), lambda i: (0, 0))
    return pl.pallas_call(
        _oproj_kernel,
        grid=(t // tm,),
        in_specs=[
            pl.BlockSpec((tm, d), lambda i: (i, 0)),
            pl.BlockSpec((tm, d), lambda i: (i, 0)),
            pl.BlockSpec((None, 1, d), lambda i: (i // per_seq, 0, 0)),
            pl.BlockSpec((None, d, d), lambda i: (layer, 0, 0)),
            vec(), vec(),
        ],
        out_specs=pl.BlockSpec((tm, d), lambda i: (i, 0)),
        out_shape=jax.ShapeDtypeStruct((t, d), F32),
        compiler_params=_params(("arbitrary",)),
        name="oproj",
    )(a, x, g1, w, lng, lnb)


def _swiglu_chunk(hb, wa_ref, wb_ref, w2_ref):
    a = jnp.dot(hb, wa_ref[...], preferred_element_type=F32)
    b = jnp.dot(hb, wb_ref[...], preferred_element_type=F32)
    g = (jax.nn.silu(a) * b).astype(BF16)
    return jnp.dot(g, w2_ref[...], preferred_element_type=F32)


def _ffn_kernel(x_ref, sc_ref, sh_ref, g2_ref, wa_ref, wb_ref, w2_ref, lng_ref, lnb_ref, o_ref,
                hb_ref, acc_ref):
    j = pl.program_id(1)

    @pl.when(j == 0)
    def _():
        hb_ref[...] = _modulate(x_ref[...], sc_ref[...], sh_ref[...]).astype(BF16)
        acc_ref[...] = jnp.zeros_like(acc_ref)

    acc_ref[...] += _swiglu_chunk(hb_ref[...], wa_ref, wb_ref, w2_ref)

    @pl.when(j == pl.num_programs(1) - 1)
    def _():
        y = ALPHA * x_ref[...] + (1.0 + g2_ref[...]) * acc_ref[...]
        o_ref[...] = _layer_norm(y, lng_ref[...], lnb_ref[...])


def _ffn(x, sc, sh, g2, w13, w2, layer, lng, lnb, seq):
    t, d = x.shape
    f = w2.shape[1]
    tm, fc = ROW_TILE, FFN_CHUNK
    n_fc = f // fc
    per_seq = seq // tm
    mod = lambda: pl.BlockSpec((None, 1, d), lambda i, j: (i // per_seq, 0, 0))
    vec = lambda: pl.BlockSpec((1, d), lambda i, j: (0, 0))
    return pl.pallas_call(
        _ffn_kernel,
        grid=(t // tm, n_fc),
        in_specs=[
            pl.BlockSpec((tm, d), lambda i, j: (i, 0)),
            mod(), mod(), mod(),
            pl.BlockSpec((None, d, fc), lambda i, j: (layer, 0, j)),
            pl.BlockSpec((None, d, fc), lambda i, j: (layer, 0, j + n_fc)),
            pl.BlockSpec((None, fc, d), lambda i, j: (layer, j, 0)),
            vec(), vec(),
        ],
        out_specs=pl.BlockSpec((tm, d), lambda i, j: (i, 0)),
        out_shape=jax.ShapeDtypeStruct((t, d), F32),
        scratch_shapes=[pltpu.VMEM((tm, d), BF16), pltpu.VMEM((tm, d), F32)],
        compiler_params=_params(("arbitrary", "arbitrary")),
        name="ffn",
    )(x, sc, sh, g2, w13, w13, w2, lng, lnb)


def _router_kernel(x_ref, sc_ref, sh_ref, rwt_ref, rb_ref, h_ref, rows_ref, cols_ref):
    h = _modulate(x_ref[...], sc_ref[...], sh_ref[...])
    h_ref[...] = h
    lg = lax.dot_general(rwt_ref[...], h, (((1,), (1,)), ((), ())), precision=HIGHEST,
                         preferred_element_type=F32) + rb_ref[...]
    idx = lax.broadcasted_iota(jnp.int32, lg.shape, 0)
    m1 = jnp.max(lg, axis=0, keepdims=True)
    i1 = jnp.min(jnp.where(lg == m1, idx, N_EXPERTS), axis=0, keepdims=True)
    lg2 = jnp.where(idx == i1, -jnp.inf, lg)
    m2 = jnp.max(lg2, axis=0, keepdims=True)
    i2 = jnp.min(jnp.where(lg2 == m2, idx, N_EXPERTS), axis=0, keepdims=True)
    e2 = jnp.exp(m2 - m1)
    den = 1.0 + e2
    w1 = 1.0 / den
    w2 = e2 / den
    meta = jnp.where(idx == 0, i1.astype(F32),
                     jnp.where(idx == 1, i2.astype(F32),
                               jnp.where(idx == 2, w1, jnp.where(idx == 3, w2, 0.0))))
    rows_ref[...] = meta
    wide = jnp.concatenate([meta, jnp.zeros((LANES - N_EXPERTS, meta.shape[1]), F32)], axis=0)
    cols_ref[...] = wide.T


def _router(x, sc, sh, rwt, rb, seq):
    t, d = x.shape
    tm = ROW_TILE
    per_seq = seq // tm
    mod = lambda: pl.BlockSpec((None, 1, d), lambda i: (i // per_seq, 0, 0))
    return pl.pallas_call(
        _router_kernel,
        grid=(t // tm,),
        in_specs=[
            pl.BlockSpec((tm, d), lambda i: (i, 0)),
            mod(), mod(),
            pl.BlockSpec((N_EXPERTS, d), lambda i: (0, 0)),
            pl.BlockSpec((N_EXPERTS, 1), lambda i: (0, 0)),
        ],
        out_specs=[
            pl.BlockSpec((tm, d), lambda i: (i, 0)),
            pl.BlockSpec((N_EXPERTS, tm), lambda i: (0, i)),
            pl.BlockSpec((tm, LANES), lambda i: (i, 0)),
        ],
        out_shape=[
            jax.ShapeDtypeStruct((t, d), F32),
            jax.ShapeDtypeStruct((N_EXPERTS, t), F32),
            jax.ShapeDtypeStruct((t, LANES), F32),
        ],
        compiler_params=_params(("arbitrary",)),
        name="router",
    )(x, sc, sh, rwt, rb)


def _moe_kernel(te_ref, nv_ref, dcur_ref, dnext_ref, dprev_ref, h_hbm, wa_ref, wb_ref, w2_ref, y_hbm,
                hbuf, hb_ref, obuf, gsem, ssem, *, n_tok, n_fc):
    i = pl.program_id(0)
    j = pl.program_id(1)
    n_tiles = pl.num_programs(0)
    tm = hbuf.shape[0]
    rows_per_step = tm // n_fc
    slot = i % 2
    prev_slot = 1 - slot

    def gather_copy(tok, r):
        return pltpu.make_async_copy(h_hbm.at[pl.ds(tok, 1), :], hbuf.at[pl.ds(r, 1), :], gsem.at[0])

    def scatter_copy(r, dst):
        return pltpu.make_async_copy(obuf.at[prev_slot, pl.ds(r, 1), :], y_hbm.at[pl.ds(dst, 1), :],
                                     ssem.at[0])

    def wait_gather():
        pltpu.make_async_copy(hbuf, hbuf, gsem.at[0]).wait()

    def wait_scatter():
        pltpu.make_async_copy(obuf.at[0], obuf.at[0], ssem.at[0]).wait()

    @pl.when(j == 0)
    def _():
        @pl.when(i == 0)
        def _():
            obuf[...] = jnp.zeros_like(obuf)

            def body(r, carry):
                gather_copy(dcur_ref[0, r] & (n_tok - 1), r).start()
                return carry
            lax.fori_loop(0, tm, body, 0)

        wait_gather()
        hb_ref[...] = hbuf[...].astype(BF16)

        @pl.when(i >= 1)
        def _():
            wait_scatter()

    def issue_row_dmas():
        base = j * rows_per_step
        dump = 2 * n_tok + prev_slot * tm
        for r in range(rows_per_step):
            row = base + r
            gather_copy(dnext_ref[0, row] & (n_tok - 1), row).start()
            dst = dprev_ref[0, row]
            scatter_copy(row, jnp.where(dst < 0, dump + row, dst)).start()

    used = nv_ref[i] > 0

    @pl.when(used)
    def _():
        issue_row_dmas()
        contrib = _swiglu_chunk(hb_ref[...], wa_ref, wb_ref, w2_ref)

        @pl.when(j == 0)
        def _():
            obuf[slot] = contrib

        @pl.when(j > 0)
        def _():
            obuf[slot] += contrib

    @pl.when(jnp.logical_not(used))
    def _():
        issue_row_dmas()

    @pl.when((j == n_fc - 1) & (i == n_tiles - 1))
    def _():
        wait_gather()
        wait_scatter()


def _moe(h, tile_expert, tile_nvalid, row_dst, w13, w2, layer):
    t, d = h.shape
    n_tiles, _, tm = row_dst.shape
    f = w2.shape[2]
    fc = MOE_CHUNK
    n_fc = f // fc
    prev_dst = jnp.concatenate([jnp.full((1, 1, tm), -1, jnp.int32), row_dst[:-1]], axis=0)

    def chunk(i, j, nv):
        return jnp.where(nv[i] > 0, j, n_fc - 1)

    grid_spec = pltpu.PrefetchScalarGridSpec(
        num_scalar_prefetch=2,
        grid=(n_tiles, n_fc),
        in_specs=[
            pl.BlockSpec((None, 1, tm), lambda i, j, te, nv: (i, 0, 0), memory_space=pltpu.SMEM),
            pl.BlockSpec((None, 1, tm), lambda i, j, te, nv: (jnp.minimum(i + 1, n_tiles - 1), 0, 0),
                         memory_space=pltpu.SMEM),
            pl.BlockSpec((None, 1, tm), lambda i, j, te, nv: (i, 0, 0), memory_space=pltpu.SMEM),
            pl.BlockSpec(memory_space=pl.ANY),
            pl.BlockSpec((None, None, d, fc), lambda i, j, te, nv: (layer, te[i], 0, chunk(i, j, nv))),
            pl.BlockSpec((None, None, d, fc), lambda i, j, te, nv: (layer, te[i], 0, chunk(i, j, nv) + n_fc)),
            pl.BlockSpec((None, None, fc, d), lambda i, j, te, nv: (layer, te[i], chunk(i, j, nv), 0)),
        ],
        out_specs=pl.BlockSpec(memory_space=pl.ANY),
        scratch_shapes=[
            pltpu.VMEM((tm, d), F32),
            pltpu.VMEM((tm, d), BF16),
            pltpu.VMEM((2, tm, d), F32),
            pltpu.SemaphoreType.DMA((1,)),
            pltpu.SemaphoreType.DMA((1,)),
        ],
    )
    return pl.pallas_call(
        functools.partial(_moe_kernel, n_tok=t, n_fc=n_fc),
        grid_spec=grid_spec,
        out_shape=jax.ShapeDtypeStruct((2 * t + 2 * tm, d), F32),
        compiler_params=_params(("arbitrary", "arbitrary")),
        name="moe",
    )(tile_expert, tile_nvalid, row_dst, row_dst, prev_dst, h, w13, w13, w2)


def _route_rows(e1, e2, tm, n_tiles):
    t = e1.shape[0]
    experts = jnp.concatenate([e1, e2])
    order = jnp.argsort(experts, stable=True).astype(jnp.int32)
    counts = jnp.sum(experts[:, None] == jnp.arange(N_EXPERTS)[None, :], axis=0).astype(jnp.int32)
    tiles_per = (counts + tm - 1) // tm
    tile_end = jnp.cumsum(tiles_per)
    tile_start = tile_end - tiles_per
    group_start = jnp.cumsum(counts) - counts
    tile_ids = jnp.arange(n_tiles, dtype=jnp.int32)
    te = jnp.minimum(jnp.sum(tile_ids[:, None] >= tile_end[None, :], axis=1), N_EXPERTS - 1).astype(jnp.int32)
    used = tile_ids < tile_end[-1]
    first_row = (tile_ids - tile_start[te]) * tm
    nvalid = jnp.where(used, jnp.clip(counts[te] - first_row, 0, tm), 0).astype(jnp.int32)
    last_e = te[jnp.maximum(tile_end[-1] - 1, 0)]
    te = jnp.where(used, te, last_e)
    r = jnp.arange(tm, dtype=jnp.int32)
    src = group_start[te][:, None] + first_row[:, None] + r[None, :]
    valid = r[None, :] < nvalid[:, None]
    row_dst = jnp.where(valid, order[jnp.clip(src, 0, 2 * t - 1)], -1).astype(jnp.int32)
    return te, nvalid, row_dst.reshape(n_tiles, 1, tm)


def _combine_kernel(x_ref, y0_ref, y1_ref, gw_ref, g2_ref, lng_ref, lnb_ref, o_ref):
    gw = gw_ref[...]
    f = gw[:, 2:3] * y0_ref[...] + gw[:, 3:4] * y1_ref[...]
    y = ALPHA * x_ref[...] + (1.0 + g2_ref[...]) * f
    o_ref[...] = _layer_norm(y, lng_ref[...], lnb_ref[...])


def _combine(x, y, gw, g2, lng, lnb, seq):
    t, d = x.shape
    tm = ROW_TILE
    per_seq = seq // tm
    n_t = t // tm
    vec = lambda: pl.BlockSpec((1, d), lambda i: (0, 0))
    return pl.pallas_call(
        _combine_kernel,
        grid=(n_t,),
        in_specs=[
            pl.BlockSpec((tm, d), lambda i: (i, 0)),
            pl.BlockSpec((tm, d), lambda i: (i, 0)),
            pl.BlockSpec((tm, d), lambda i: (i + n_t, 0)),
            pl.BlockSpec((tm, LANES), lambda i: (i, 0)),
            pl.BlockSpec((None, 1, d), lambda i: (i // per_seq, 0, 0)),
            vec(), vec(),
        ],
        out_specs=pl.BlockSpec((tm, d), lambda i: (i, 0)),
        out_shape=jax.ShapeDtypeStruct((t, d), F32),
        compiler_params=_params(("arbitrary",)),
        name="combine",
    )(x, y, y, gw, g2, lng, lnb)


def _kv_kernel(x_ref, sc_ref, sh_ref, w_ref, ka_ref, va_ref, km_ref, *, blocks_per_seq):
    h = _modulate(x_ref[...], sc_ref[...], sh_ref[...]).astype(BF16)
    kv = jnp.dot(h, w_ref[...], preferred_element_type=F32)
    k = kv[:, :D_MODEL]
    v = kv[:, D_MODEL:]
    km_ref[...] = jnp.mean(k, axis=0, keepdims=True)
    j = pl.program_id(0) % blocks_per_seq
    shape = (MOBA_BLOCK, HEAD_DIM)
    col = lax.broadcasted_iota(jnp.int32, shape, 1)
    pos = lax.broadcasted_iota(jnp.int32, shape, 0).astype(F32)
    blk = (j * MOBA_BLOCK).astype(F32)
    kx = jnp.where(col < 3, pos,
                   jnp.where(col < 6, blk,
                             jnp.where((col >= 8) & (col < 11), 1.0,
                                       jnp.where(col == 16 + j, 1.0, 0.0)))).astype(BF16)
    vx = jnp.where(col == 0, 1.0, 0.0).astype(BF16)
    for hd in range(N_HEADS):
        lo = hd * HEAD_DIM
        ka_ref[hd] = jnp.concatenate([k[:, lo:lo + HEAD_DIM].astype(BF16), kx], axis=1)
        va_ref[hd] = jnp.concatenate([v[:, lo:lo + HEAD_DIM].astype(BF16), vx], axis=1)


def _shared_kv(x, sc, sh, w, batch, seq):
    t, d = x.shape
    tm = MOBA_BLOCK
    n_blk = seq // tm
    mod = lambda: pl.BlockSpec((None, 1, d), lambda i: (i // n_blk, 0, 0))
    aug = lambda: pl.BlockSpec((None, N_HEADS, tm, AUG), lambda i: (i // n_blk, 0, i % n_blk, 0))
    return pl.pallas_call(
        functools.partial(_kv_kernel, blocks_per_seq=n_blk),
        grid=(t // tm,),
        in_specs=[
            pl.BlockSpec((tm, d), lambda i: (i, 0)),
            mod(), mod(),
            pl.BlockSpec((d, 2 * d), lambda i: (0, 0)),
        ],
        out_specs=[aug(), aug(), pl.BlockSpec((None, None, 1, d), lambda i: (i // n_blk, i % n_blk, 0, 0))],
        out_shape=[
            jax.ShapeDtypeStruct((batch, N_HEADS, seq, AUG), BF16),
            jax.ShapeDtypeStruct((batch, N_HEADS, seq, AUG), BF16),
            jax.ShapeDtypeStruct((batch, n_blk, 1, d), F32),
        ],
        compiler_params=_params(("arbitrary",)),
        name="shared_kv",
    )(x, sc, sh, w)


def _split3(x):
    p1 = x.astype(BF16).astype(F32)
    p2 = (x - p1).astype(BF16).astype(F32)
    p3 = (x - p1 - p2).astype(BF16).astype(F32)
    return p1, p2, p3


def _dot_nt(a, b):
    return lax.dot_general(a, b, (((1,), (1,)), ((), ())), preferred_element_type=F32)


def _qaug_kernel(sl_ref, x_ref, sc_ref, sh_ref, w_ref, kmbd_ref, qa_ref, *, n_blk):
    i = pl.program_id(0) % n_blk
    bq = MOBA_BLOCK
    h = _modulate(x_ref[...], sc_ref[...], sh_ref[...]).astype(BF16)
    qb = (jnp.dot(h, w_ref[...], preferred_element_type=F32) * (HEAD_DIM ** -0.5 * LOG2E)).astype(BF16)
    gates = _dot_nt(kmbd_ref[0], qb) + _dot_nt(kmbd_ref[1], qb) + _dot_nt(kmbd_ref[2], qb)
    lane_q = lax.broadcasted_iota(jnp.int32, (SUBLANES, bq), 1).astype(F32)
    sub8 = lax.broadcasted_iota(jnp.int32, (SUBLANES, bq), 0)
    blk_id = lax.broadcasted_iota(jnp.int32, (n_blk, bq), 0)
    t_q = lane_q + (i * bq).astype(F32)
    group = LANES // EXT
    for g0 in range(0, N_HEADS, group):
        exts = []
        for hd in range(g0, g0 + group):
            gate = jnp.where(blk_id < i, gates[hd * n_blk:(hd + 1) * n_blk, :], -jnp.inf)
            rank = jnp.zeros((n_blk, bq), jnp.int32)
            for m in range(n_blk):
                gm = gate[m:m + 1, :]
                ahead = (gm > gate) | ((gm == gate) & (blk_id > m))
                rank = rank + ahead.astype(jnp.int32)
            bias = jnp.where((blk_id < i) & (rank >= MOBA_TOPK), NEG_INF, 0.0)
            slope = jnp.full((SUBLANES, bq), sl_ref[hd], F32) * LOG2E
            s1, s2, s3 = _split3(slope)
            t1, t2, t3 = _split3(-slope * t_q)
            rows_s = jnp.where((sub8 == 0) | (sub8 == 3), s1,
                               jnp.where((sub8 == 1) | (sub8 == 4), s2,
                                         jnp.where((sub8 == 2) | (sub8 == 5), s3, 0.0)))
            rows_t = jnp.where(sub8 == 0, t1, jnp.where(sub8 == 1, t2, jnp.where(sub8 == 2, t3, 0.0)))
            pieces = [rows_s, rows_t, bias]
            if EXT - 2 * SUBLANES - n_blk:
                pieces.append(jnp.zeros((EXT - 2 * SUBLANES - n_blk, bq), F32))
            exts.extend(pieces)
        ext_t = jnp.concatenate(exts, axis=0).T.astype(BF16)
        pad = jnp.zeros((bq, AUG - HEAD_DIM - EXT), BF16)
        for k, hd in enumerate(range(g0, g0 + group)):
            qa_ref[hd] = jnp.concatenate([qb[:, hd * HEAD_DIM:(hd + 1) * HEAD_DIM],
                                          ext_t[:, k * EXT:(k + 1) * EXT], pad], axis=1)


def _q_aug(slopes, x, sc, sh, w, layer, kmbd, batch, seq):
    t, d = x.shape
    tm = MOBA_BLOCK
    n_blk = seq // tm
    rows = kmbd.shape[2]
    mod = lambda: pl.BlockSpec((None, 1, d), lambda i, sl: (i // n_blk, 0, 0))
    grid_spec = pltpu.PrefetchScalarGridSpec(
        num_scalar_prefetch=1,
        grid=(t // tm,),
        in_specs=[
            pl.BlockSpec((tm, d), lambda i, sl: (i, 0)),
            mod(), mod(),
            pl.BlockSpec((None, d, d), lambda i, sl: (layer, 0, 0)),
            pl.BlockSpec((None, 3, rows, d), lambda i, sl: (i // n_blk, 0, 0, 0)),
        ],
        out_specs=pl.BlockSpec((None, N_HEADS, tm, AUG), lambda i, sl: (i // n_blk, 0, i % n_blk, 0)),
    )
    return pl.pallas_call(
        functools.partial(_qaug_kernel, n_blk=n_blk),
        grid_spec=grid_spec,
        out_shape=jax.ShapeDtypeStruct((batch, N_HEADS, seq, AUG), BF16),
        compiler_params=_params(("arbitrary",)),
        name="q_aug",
    )(slopes, x, sc, sh, w, kmbd)


def _block_diag_means(km, n_blk):
    d = km.shape[-1]
    r = jnp.arange(N_HEADS * n_blk)
    head_of_col = jnp.arange(d) // HEAD_DIM
    bd = jnp.where((r // n_blk)[None, :, None] == head_of_col[None, None, :], km[:, r % n_blk, :], 0.0)
    return jnp.stack(_split3(bd), axis=1).astype(BF16)


def _attn_kernel(qlo_ref, qhi_ref, ka_ref, va_ref, olo_ref, ohi_ref, q_scr, s_scr, mx_scr, acc_scr, *, n_blk):
    i = pl.program_id(2)
    bq = MOBA_BLOCK
    n_units = n_blk + 1
    q_scr[0] = qlo_ref[...]
    q_scr[1] = qhi_ref[...]
    mx_scr[...] = jnp.full(mx_scr.shape, NEG_INF, F32)
    acc_scr[...] = jnp.zeros(acc_scr.shape, F32)
    dmat = lax.broadcasted_iota(jnp.int32, (bq, bq), 1) - lax.broadcasted_iota(jnp.int32, (bq, bq), 0)

    def unit(t):
        is_lo = i >= t
        sel = jnp.where(is_lo, 0, 1)
        key = jnp.where(is_lo, t, t - i - 1)
        return is_lo, sel, pl.multiple_of(key * bq, bq)

    for t in range(n_units):
        is_lo, sel, start = unit(t)
        own_possible = t < n_blk // 2 or t == n_blk
        if own_possible:
            causal = dmat <= jnp.where(is_lo, i - t, n_blk - t) * bq
        for hh in range(HEAD_PAIR):
            s = _dot_nt(q_scr[sel, hh], ka_ref[hh, pl.ds(start, bq), :])
            if own_possible:
                s = jnp.where(causal, s, NEG_INF)
            s_scr[hh, t] = s
            mx_scr[sel, hh] = jnp.maximum(mx_scr[sel, hh], jnp.maximum(s[:, :LANES], s[:, LANES:]))

    for sel in range(2):
        for hh in range(HEAD_PAIR):
            m = jnp.max(mx_scr[sel, hh], axis=1, keepdims=True)
            mx_scr[sel, hh] = jnp.broadcast_to(m, (bq, LANES))

    for t in range(n_units):
        _, sel, start = unit(t)
        for hh in range(HEAD_PAIR):
            m_row = mx_scr[sel, hh]
            p = jnp.exp2(s_scr[hh, t] - jnp.concatenate([m_row, m_row], axis=1)).astype(BF16)
            acc_scr[sel, hh] += jnp.dot(p, va_ref[hh, pl.ds(start, bq), :], preferred_element_type=F32)

    for sel, o_ref in enumerate((olo_ref, ohi_ref)):
        outs = []
        for hh in range(HEAD_PAIR):
            acc = acc_scr[sel, hh]
            outs.append(acc[:, :HEAD_DIM] / acc[:, HEAD_DIM:HEAD_DIM + 1])
        o_ref[...] = jnp.concatenate(outs, axis=1).astype(o_ref.dtype)


def _attention(qa, ka, va, batch, seq):
    n_blk = seq // MOBA_BLOCK
    assert n_blk % 2 == 0
    pair = lambda: pl.BlockSpec((None, HEAD_PAIR, seq, AUG), lambda b, hp, i: (b, hp, 0, 0))
    half = n_blk // 2
    o_shape = jax.ShapeDtypeStruct((batch, seq // 2, D_MODEL), BF16)
    out_lo, out_hi = pl.pallas_call(
        functools.partial(_attn_kernel, n_blk=n_blk),
        grid=(batch, N_HEADS // HEAD_PAIR, n_blk // 2),
        in_specs=[
            pl.BlockSpec((None, HEAD_PAIR, MOBA_BLOCK, AUG), lambda b, hp, i: (b, hp, i, 0)),
            pl.BlockSpec((None, HEAD_PAIR, MOBA_BLOCK, AUG), lambda b, hp, i: (b, hp, n_blk - 1 - i, 0)),
            pair(), pair(),
        ],
        out_specs=[
            pl.BlockSpec((None, MOBA_BLOCK, HEAD_PAIR * HEAD_DIM), lambda b, hp, i: (b, i, hp)),
            pl.BlockSpec((None, MOBA_BLOCK, HEAD_PAIR * HEAD_DIM), lambda b, hp, i: (b, half - 1 - i, hp)),
        ],
        out_shape=[o_shape, o_shape],
        scratch_shapes=[
            pltpu.VMEM((2, HEAD_PAIR, MOBA_BLOCK, AUG), BF16),
            pltpu.VMEM((HEAD_PAIR, n_blk + 1, MOBA_BLOCK, MOBA_BLOCK), F32),
            pltpu.VMEM((2, HEAD_PAIR, MOBA_BLOCK, LANES), F32),
            pltpu.VMEM((2, HEAD_PAIR, MOBA_BLOCK, AUG), F32),
        ],
        compiler_params=_params(("arbitrary", "arbitrary", "arbitrary")),
        name="moba_attn",
    )(qa, qa, ka, va)
    return jnp.concatenate([out_lo, out_hi], axis=1)


def kernel(x, c, ada_w, ada_b, ln_g, ln_b, conv_in_w, conv_in_b, conv_dw_w, conv_dw_b, conv_norm_g,
           conv_norm_b, conv_out_w, conv_out_b, kv_ada_w, kv_ada_b, w_kv, w_q, w_o, ffn_w13, ffn_w2,
           router_w, router_b, moe_w13, moe_w2):
    batch, seq, d = x.shape
    t = batch * seq
    assert d == D_MODEL and seq % ROW_TILE == 0 and seq % MOBA_BLOCK == 0 and t & (t - 1) == 0
    n_blk = seq // MOBA_BLOCK
    assert n_blk <= 16

    c_pad = jnp.pad(c, ((0, SUBLANES - batch), (0, 0)))
    mods = _ada(c_pad, ada_w, ada_b, 1536)[:, :batch]
    kv_mod = _ada(c_pad, kv_ada_w[None], kv_ada_b[None], 1024)[0, :batch]

    def mod_vec(v):
        return v.reshape(batch, 1, d)

    row = lambda v: v.reshape(1, -1)
    slopes = jnp.exp2(-8.0 * jnp.arange(1, N_HEADS + 1, dtype=F32) / N_HEADS)
    n_moe_tiles = 2 * t // MOE_TILE + N_EXPERTS
    conv_in_wb, conv_out_wb, w_qb, w_ob = [w.astype(BF16) for w in (conv_in_w, conv_out_w, w_q, w_o)]
    ffn_w13b, ffn_w2b, moe_w13b, moe_w2b = [w.astype(BF16) for w in (ffn_w13, ffn_w2, moe_w13, moe_w2)]

    xf = x.reshape(t, d)
    ka = va = kmbd = None
    for l in range(DEPTH):
        if l == N_A_LAYERS:
            ka, va, km = _shared_kv(xf, mod_vec(kv_mod[:, d:]), mod_vec(kv_mod[:, :d]),
                                    w_kv.astype(BF16), batch, seq)
            kmbd = _block_diag_means(km.reshape(batch, n_blk, d), n_blk)
        sh1, sc1, g1, sh2, sc2, g2 = [mod_vec(mods[l, :, k * d:(k + 1) * d]) for k in range(6)]
        lng1, lnb1 = row(ln_g[l, 0]), row(ln_b[l, 0])
        lng2, lnb2 = row(ln_g[l, 1]), row(ln_b[l, 1])
        if l < N_A_LAYERS:
            u = _conv_in(xf, sc1, sh1, conv_in_wb, l, row(conv_in_b[l]), seq)
            xf = _conv_block(u, xf, g1, conv_dw_w[l], row(conv_dw_b[l]), row(conv_norm_g[l]),
                             row(conv_norm_b[l]), conv_out_wb, l, row(conv_out_b[l]), lng1, lnb1, seq)
        else:
            jl = l - N_A_LAYERS
            qa = _q_aug(slopes, xf, sc1, sh1, w_qb, jl, kmbd, batch, seq)
            att = _attention(qa, ka, va, batch, seq)
            xf = _oproj(att.reshape(t, d), xf, g1, w_ob, jl, lng1, lnb1, seq)
        if l % 2 == 0:
            xf = _ffn(xf, sc2, sh2, g2, ffn_w13b, ffn_w2b, l // 2, lng2, lnb2, seq)
        else:
            e = l // 2
            h, meta_rows, meta_cols = _router(xf, sc2, sh2, router_w[e].T, router_b[e].reshape(-1, 1), seq)
            te, nvalid, row_dst = _route_rows(meta_rows[0].astype(jnp.int32), meta_rows[1].astype(jnp.int32),
                                              MOE_TILE, n_moe_tiles)
            y = _moe(h, te, nvalid, row_dst, moe_w13b, moe_w2b, e)
            xf = _combine(xf, y, meta_cols, g2, lng2, lnb2, seq)
    return xf.reshape(batch, seq, d)
```

```python
import functools

import jax
import jax.numpy as jnp
from jax import lax
from jax.experimental import pallas as pl
from jax.experimental.pallas import tpu as pltpu

F32 = jnp.float32
BF16 = jnp.bfloat16
HIGHEST = lax.Precision.HIGHEST

D_MODEL = 1024
DEPTH = 4
N_A_LAYERS = DEPTH // 2
CONV_K = 31
N_HEADS = 16
HEAD_DIM = D_MODEL // N_HEADS
MOBA_BLOCK = 256
MOBA_TOPK = 3
D_FF = 2816
N_EXPERTS = 8
D_FF_EXPERT = 3584
ALPHA = (2.0 * DEPTH) ** 0.25
LN_EPS = 1e-5
NEG_INF = -1e30

SUBLANES = 8
LANES = 128
VMEM_LIMIT = 48 * 1024 * 1024

ROW_TILE = 512
CONV_HALO = 32
CONV_TILE = 256
CONV_CHUNK = 32
FFN_CHUNK = 1408
MOE_CHUNK = 1792
MOE_TILE = 512
HEAD_PAIR = 2
AUG = 2 * HEAD_DIM
EXT = 32
LOG2E = 1.4426950408889634


def _params(sem, vmem=VMEM_LIMIT):
    return pltpu.CompilerParams(dimension_semantics=sem, vmem_limit_bytes=vmem)


def _layer_norm(y, g, b):
    mu = jnp.mean(y, axis=-1, keepdims=True)
    yc = y - mu
    var = jnp.mean(yc * yc, axis=-1, keepdims=True)
    return yc * lax.rsqrt(var + LN_EPS) * g + b


def _modulate(x, sc, sh):
    return x * (1.0 + sc) + sh


def _ada_kernel(c_ref, w_ref, b_ref, o_ref):
    ca = jax.nn.silu(c_ref[...])
    o_ref[...] = jnp.dot(ca, w_ref[...], precision=HIGHEST, preferred_element_type=F32) + b_ref[...]


def _ada(c_pad, w, b, tn):
    n_l, d, n = w.shape
    rows = c_pad.shape[0]
    return pl.pallas_call(
        _ada_kernel,
        grid=(n_l, n // tn),
        in_specs=[
            pl.BlockSpec((rows, d), lambda l, j: (0, 0)),
            pl.BlockSpec((None, d, tn), lambda l, j: (l, 0, j)),
            pl.BlockSpec((None, 1, tn), lambda l, j: (l, 0, j)),
        ],
        out_specs=pl.BlockSpec((None, rows, tn), lambda l, j: (l, 0, j)),
        out_shape=jax.ShapeDtypeStruct((n_l, rows, n), F32),
        compiler_params=_params(("arbitrary", "arbitrary")),
        name="ada",
    )(c_pad, w, b.reshape(n_l, 1, n))


def _conv_in_kernel(x_ref, sc_ref, sh_ref, w_ref, b_ref, o_ref):
    h = _modulate(x_ref[...], sc_ref[...], sh_ref[...]).astype(BF16)
    y = jnp.dot(h, w_ref[...], preferred_element_type=F32) + b_ref[...]
    half = y.shape[1] // 2
    o_ref[...] = y[:, :half] * jax.nn.sigmoid(y[:, half:])


def _conv_in(x, sc, sh, w, layer, b, seq):
    t, d = x.shape
    n = w.shape[2]
    n_out = n // 2
    tm = ROW_TILE
    per_seq = seq // tm
    return pl.pallas_call(
        _conv_in_kernel,
        grid=(t // tm,),
        in_specs=[
            pl.BlockSpec((tm, d), lambda i: (i, 0)),
            pl.BlockSpec((None, 1, d), lambda i: (i // per_seq, 0, 0)),
            pl.BlockSpec((None, 1, d), lambda i: (i // per_seq, 0, 0)),
            pl.BlockSpec((None, d, n), lambda i: (layer, 0, 0)),
            pl.BlockSpec((1, n), lambda i: (0, 0)),
        ],
        out_specs=pl.BlockSpec((tm, n_out), lambda i: (i, 0)),
        out_shape=jax.ShapeDtypeStruct((t, n_out), F32),
        compiler_params=_params(("arbitrary",)),
        name="conv_in",
    )(x, sc, sh, w, b)


def _conv_kernel(uprev_ref, u_ref, x_ref, g1_ref, wdw_ref, bdw_ref, gn_ref, bn_ref, wo_ref, bo_ref,
                 lng_ref, lnb_ref, o_ref, win_ref, cv_ref, *, per_seq):
    tm = u_ref.shape[0]
    rows = CONV_HALO + tm
    first = (pl.program_id(0) % per_seq) == 0
    win_ref[0, 0:CONV_HALO, :] = jnp.where(first, 0.0, uprev_ref[...])
    win_ref[0, CONV_HALO:, :] = u_ref[...]
    for b in range(1, SUBLANES):
        win_ref[b, SUBLANES:, :] = win_ref[0, SUBLANES - b:rows - b, :]
    for c in range(tm // CONV_CHUNK):
        base = CONV_HALO + c * CONV_CHUNK
        acc = jnp.zeros((CONV_CHUNK, D_MODEL), F32) + bdw_ref[...]
        for s in range(CONV_K):
            a, b = divmod(s, SUBLANES)
            k = CONV_K - 1 - s
            acc = acc + win_ref[b, base - SUBLANES * a:base - SUBLANES * a + CONV_CHUNK, :] * wdw_ref[k:k + 1, :]
        cv_ref[c * CONV_CHUNK:(c + 1) * CONV_CHUNK, :] = acc
    un = jax.nn.silu(_layer_norm(cv_ref[...], gn_ref[...], bn_ref[...])).astype(BF16)
    out = jnp.dot(un, wo_ref[...], preferred_element_type=F32) + bo_ref[...]
    y = ALPHA * x_ref[...] + (1.0 + g1_ref[...]) * out
    o_ref[...] = _layer_norm(y, lng_ref[...], lnb_ref[...])


def _conv_block(u, x, g1, wdw, bdw, gn, bn, wo, layer, bo, lng, lnb, seq):
    t, d = x.shape
    tm = CONV_TILE
    per_seq = seq // tm
    halo_per_tile = tm // CONV_HALO
    vec = lambda: pl.BlockSpec((1, d), lambda i: (0, 0))
    return pl.pallas_call(
        functools.partial(_conv_kernel, per_seq=per_seq),
        grid=(t // tm,),
        in_specs=[
            pl.BlockSpec((CONV_HALO, d), lambda i: (jnp.maximum(i * halo_per_tile - 1, 0), 0)),
            pl.BlockSpec((tm, d), lambda i: (i, 0)),
            pl.BlockSpec((tm, d), lambda i: (i, 0)),
            pl.BlockSpec((None, 1, d), lambda i: (i // per_seq, 0, 0)),
            pl.BlockSpec((CONV_K, d), lambda i: (0, 0)),
            vec(), vec(), vec(),
            pl.BlockSpec((None, d, d), lambda i: (layer, 0, 0)),
            vec(), vec(), vec(),
        ],
        out_specs=pl.BlockSpec((tm, d), lambda i: (i, 0)),
        out_shape=jax.ShapeDtypeStruct((t, d), F32),
        scratch_shapes=[pltpu.VMEM((SUBLANES, CONV_HALO + tm, d), F32), pltpu.VMEM((tm, d), F32)],
        compiler_params=_params(("arbitrary",)),
        name="conv_block",
    )(u, u, x, g1, wdw, bdw, gn, bn, wo, bo, lng, lnb)


def _oproj_kernel(a_ref, x_ref, g1_ref, w_ref, lng_ref, lnb_ref, o_ref):
    out = jnp.dot(a_ref[...], w_ref[...], preferred_element_type=F32)
    y = ALPHA * x_ref[...] + (1.0 + g1_ref[...]) * out
    o_ref[...] = _layer_norm(y, lng_ref[...], lnb_ref[...])


def _oproj(a, x, g1, w, layer, lng, lnb, seq):
    t, d = x.shape
    tm = ROW_TILE
    per_seq = seq // tm
    vec = lambda: pl.BlockSpec((1, d), lambda i: (0, 0))
    return pl.pallas_call(
        _oproj_kernel,
        grid=(t // tm,),
        in_specs=[
            pl.BlockSpec((tm, d), lambda i: (i, 0)),
            pl.BlockSpec((tm, d), lambda i: (i, 0)),
            pl.BlockSpec((None, 1, d), lambda i: (i // per_seq, 0, 0)),
            pl.BlockSpec((None, d, d), lambda i: (layer, 0, 0)),
            vec(), vec(),
        ],
        out_specs=pl.BlockSpec((tm, d), lambda i: (i, 0)),
        out_shape=jax.ShapeDtypeStruct((t, d), F32),
        compiler_params=_params(("arbitrary",)),
        name="oproj",
    )(a, x, g1, w, lng, lnb)


def _swiglu_chunk(hb, wa_ref, wb_ref, w2_ref):
    a = jnp.dot(hb, wa_ref[...], preferred_element_type=F32)
    b = jnp.dot(hb, wb_ref[...], preferred_element_type=F32)
    g = (jax.nn.silu(a) * b).astype(BF16)
    return jnp.dot(g, w2_ref[...], preferred_element_type=F32)


def _ffn_kernel(x_ref, sc_ref, sh_ref, g2_ref, wa_ref, wb_ref, w2_ref, lng_ref, lnb_ref, o_ref,
                hb_ref, acc_ref):
    j = pl.program_id(1)

    @pl.when(j == 0)
    def _():
        hb_ref[...] = _modulate(x_ref[...], sc_ref[...], sh_ref[...]).astype(BF16)
        acc_ref[...] = jnp.zeros_like(acc_ref)

    acc_ref[...] += _swiglu_chunk(hb_ref[...], wa_ref, wb_ref, w2_ref)

    @pl.when(j == pl.num_programs(1) - 1)
    def _():
        y = ALPHA * x_ref[...] + (1.0 + g2_ref[...]) * acc_ref[...]
        o_ref[...] = _layer_norm(y, lng_ref[...], lnb_ref[...])


def _ffn(x, sc, sh, g2, w13, w2, layer, lng, lnb, seq):
    t, d = x.shape
    f = w2.shape[1]
    tm, fc = ROW_TILE, FFN_CHUNK
    n_fc = f // fc
    per_seq = seq // tm
    mod = lambda: pl.BlockSpec((None, 1, d), lambda i, j: (i // per_seq, 0, 0))
    vec = lambda: pl.BlockSpec((1, d), lambda i, j: (0, 0))
    return pl.pallas_call(
        _ffn_kernel,
        grid=(t // tm, n_fc),
        in_specs=[
            pl.BlockSpec((tm, d), lambda i, j: (i, 0)),
            mod(), mod(), mod(),
            pl.BlockSpec((None, d, fc), lambda i, j: (layer, 0, j)),
            pl.BlockSpec((None, d, fc), lambda i, j: (layer, 0, j + n_fc)),
            pl.BlockSpec((None, fc, d), lambda i, j: (layer, j, 0)),
            vec(), vec(),
        ],
        out_specs=pl.BlockSpec((tm, d), lambda i, j: (i, 0)),
        out_shape=jax.ShapeDtypeStruct((t, d), F32),
        scratch_shapes=[pltpu.VMEM((tm, d), BF16), pltpu.VMEM((tm, d), F32)],
        compiler_params=_params(("arbitrary", "arbitrary")),
        name="ffn",
    )(x, sc, sh, g2, w13, w13, w2, lng, lnb)


def _router_kernel(x_ref, sc_ref, sh_ref, rwt_ref, rb_ref, h_ref, rows_ref, cols_ref):
    h = _modulate(x_ref[...], sc_ref[...], sh_ref[...])
    h_ref[...] = h
    lg = lax.dot_general(rwt_ref[...], h, (((1,), (1,)), ((), ())), precision=HIGHEST,
                         preferred_element_type=F32) + rb_ref[...]
    idx = lax.broadcasted_iota(jnp.int32, lg.shape, 0)
    m1 = jnp.max(lg, axis=0, keepdims=True)
    i1 = jnp.min(jnp.where(lg == m1, idx, N_EXPERTS), axis=0, keepdims=True)
    lg2 = jnp.where(idx == i1, -jnp.inf, lg)
    m2 = jnp.max(lg2, axis=0, keepdims=True)
    i2 = jnp.min(jnp.where(lg2 == m2, idx, N_EXPERTS), axis=0, keepdims=True)
    e2 = jnp.exp(m2 - m1)
    den = 1.0 + e2
    w1 = 1.0 / den
    w2 = e2 / den
    meta = jnp.where(idx == 0, i1.astype(F32),
                     jnp.where(idx == 1, i2.astype(F32),
                               jnp.where(idx == 2, w1, jnp.where(idx == 3, w2, 0.0))))
    rows_ref[...] = meta
    wide = jnp.concatenate([meta, jnp.zeros((LANES - N_EXPERTS, meta.shape[1]), F32)], axis=0)
    cols_ref[...] = wide.T


def _router(x, sc, sh, rwt, rb, seq):
    t, d = x.shape
    tm = ROW_TILE
    per_seq = seq // tm
    mod = lambda: pl.BlockSpec((None, 1, d), lambda i: (i // per_seq, 0, 0))
    return pl.pallas_call(
        _router_kernel,
        grid=(t // tm,),
        in_specs=[
            pl.BlockSpec((tm, d), lambda i: (i, 0)),
            mod(), mod(),
            pl.BlockSpec((N_EXPERTS, d), lambda i: (0, 0)),
            pl.BlockSpec((N_EXPERTS, 1), lambda i: (0, 0)),
        ],
        out_specs=[
            pl.BlockSpec((tm, d), lambda i: (i, 0)),
            pl.BlockSpec((N_EXPERTS, tm), lambda i: (0, i)),
            pl.BlockSpec((tm, LANES), lambda i: (i, 0)),
        ],
        out_shape=[
            jax.ShapeDtypeStruct((t, d), F32),
            jax.ShapeDtypeStruct((N_EXPERTS, t), F32),
            jax.ShapeDtypeStruct((t, LANES), F32),
        ],
        compiler_params=_params(("arbitrary",)),
        name="router",
    )(x, sc, sh, rwt, rb)


def _moe_kernel(te_ref, nv_ref, dcur_ref, dnext_ref, dprev_ref, h_hbm, wa_ref, wb_ref, w2_ref, y_hbm,
                hbuf, hb_ref, obuf, gsem, ssem, *, n_tok, n_fc):
    i = pl.program_id(0)
    j = pl.program_id(1)
    n_tiles = pl.num_programs(0)
    _, rows_per_step, d = hbuf.shape
    tm = n_fc * rows_per_step
    slot = i % 2
    prev_slot = 1 - slot

    def gather_copy(tok, step, r):
        return pltpu.make_async_copy(h_hbm.at[pl.ds(tok, 1), :], hbuf.at[step, pl.ds(r, 1), :], gsem.at[0])

    def scatter_copy(step, r, dst):
        return pltpu.make_async_copy(obuf.at[prev_slot, step, pl.ds(r, 1), :], y_hbm.at[pl.ds(dst, 1), :],
                                     ssem.at[0])

    def wait_gather():
        pltpu.make_async_copy(hbuf, hbuf, gsem.at[0]).wait()

    def wait_scatter():
        pltpu.make_async_copy(obuf.at[0], obuf.at[0], ssem.at[0]).wait()

    @pl.when(j == 0)
    def _():
        @pl.when(i == 0)
        def _():
            obuf[...] = jnp.zeros_like(obuf)

            def body(row, carry):
                gather_copy(dcur_ref[0, row] & (n_tok - 1), row // rows_per_step, row % rows_per_step).start()
                return carry
            lax.fori_loop(0, tm, body, 0)

        wait_gather()
        hb_ref[...] = hbuf[...].reshape(tm, d).astype(BF16)

        @pl.when(i >= 1)
        def _():
            wait_scatter()

    def issue_row_dmas():
        base = j * rows_per_step
        dump = 2 * n_tok + prev_slot * tm + base
        for r in range(rows_per_step):
            gather_copy(dnext_ref[0, base + r] & (n_tok - 1), j, r).start()
            dst = dprev_ref[0, base + r]
            scatter_copy(j, r, jnp.where(dst < 0, dump + r, dst)).start()

    used = nv_ref[i] > 0

    @pl.when(used)
    def _():
        contrib = _swiglu_chunk(hb_ref[...], wa_ref, wb_ref, w2_ref).reshape(n_fc, rows_per_step, d)
        issue_row_dmas()

        @pl.when(j == 0)
        def _():
            obuf[slot] = contrib

        @pl.when(j > 0)
        def _():
            obuf[slot] += contrib

    @pl.when(jnp.logical_not(used))
    def _():
        issue_row_dmas()

    @pl.when((j == n_fc - 1) & (i == n_tiles - 1))
    def _():
        wait_gather()
        wait_scatter()


def _moe(h, tile_expert, tile_nvalid, row_dst, w13, w2, layer):
    t, d = h.shape
    n_tiles, _, tm = row_dst.shape
    f = w2.shape[2]
    fc = MOE_CHUNK
    n_fc = f // fc
    prev_dst = jnp.concatenate([jnp.full((1, 1, tm), -1, jnp.int32), row_dst[:-1]], axis=0)

    def chunk(i, j, nv):
        return jnp.where(nv[i] > 0, j, n_fc - 1)

    grid_spec = pltpu.PrefetchScalarGridSpec(
        num_scalar_prefetch=2,
        grid=(n_tiles, n_fc),
        in_specs=[
            pl.BlockSpec((None, 1, tm), lambda i, j, te, nv: (i, 0, 0), memory_space=pltpu.SMEM),
            pl.BlockSpec((None, 1, tm), lambda i, j, te, nv: (jnp.minimum(i + 1, n_tiles - 1), 0, 0),
                         memory_space=pltpu.SMEM),
            pl.BlockSpec((None, 1, tm), lambda i, j, te, nv: (i, 0, 0), memory_space=pltpu.SMEM),
            pl.BlockSpec(memory_space=pl.ANY),
            pl.BlockSpec((None, None, d, fc), lambda i, j, te, nv: (layer, te[i], 0, chunk(i, j, nv))),
            pl.BlockSpec((None, None, d, fc), lambda i, j, te, nv: (layer, te[i], 0, chunk(i, j, nv) + n_fc)),
            pl.BlockSpec((None, None, fc, d), lambda i, j, te, nv: (layer, te[i], chunk(i, j, nv), 0)),
        ],
        out_specs=pl.BlockSpec(memory_space=pl.ANY),
        scratch_shapes=[
            pltpu.VMEM((n_fc, tm // n_fc, d), F32),
            pltpu.VMEM((tm, d), BF16),
            pltpu.VMEM((2, n_fc, tm // n_fc, d), F32),
            pltpu.SemaphoreType.DMA((1,)),
            pltpu.SemaphoreType.DMA((1,)),
        ],
    )
    return pl.pallas_call(
        functools.partial(_moe_kernel, n_tok=t, n_fc=n_fc),
        grid_spec=grid_spec,
        out_shape=jax.ShapeDtypeStruct((2 * t + 2 * tm, d), F32),
        compiler_params=_params(("arbitrary", "arbitrary")),
        name="moe",
    )(tile_expert, tile_nvalid, row_dst, row_dst, prev_dst, h, w13, w13, w2)


def _route_rows(e1, e2, tm, n_tiles):
    t = e1.shape[0]
    experts = jnp.concatenate([e1, e2])
    order = jnp.argsort(experts, stable=True).astype(jnp.int32)
    counts = jnp.sum(experts[:, None] == jnp.arange(N_EXPERTS)[None, :], axis=0).astype(jnp.int32)
    tiles_per = (counts + tm - 1) // tm
    tile_end = jnp.cumsum(tiles_per)
    tile_start = tile_end - tiles_per
    group_start = jnp.cumsum(counts) - counts
    tile_ids = jnp.arange(n_tiles, dtype=jnp.int32)
    te = jnp.minimum(jnp.sum(tile_ids[:, None] >= tile_end[None, :], axis=1), N_EXPERTS - 1).astype(jnp.int32)
    used = tile_ids < tile_end[-1]
    first_row = (tile_ids - tile_start[te]) * tm
    nvalid = jnp.where(used, jnp.clip(counts[te] - first_row, 0, tm), 0).astype(jnp.int32)
    last_e = te[jnp.maximum(tile_end[-1] - 1, 0)]
    te = jnp.where(used, te, last_e)
    r = jnp.arange(tm, dtype=jnp.int32)
    src = group_start[te][:, None] + first_row[:, None] + r[None, :]
    valid = r[None, :] < nvalid[:, None]
    row_dst = jnp.where(valid, order[jnp.clip(src, 0, 2 * t - 1)], -1).astype(jnp.int32)
    return te, nvalid, row_dst.reshape(n_tiles, 1, tm)


def _combine_kernel(x_ref, y0_ref, y1_ref, gw_ref, g2_ref, lng_ref, lnb_ref, o_ref):
    gw = gw_ref[...]
    f = gw[:, 2:3] * y0_ref[...] + gw[:, 3:4] * y1_ref[...]
    y = ALPHA * x_ref[...] + (1.0 + g2_ref[...]) * f
    o_ref[...] = _layer_norm(y, lng_ref[...], lnb_ref[...])


def _combine(x, y, gw, g2, lng, lnb, seq):
    t, d = x.shape
    tm = ROW_TILE
    per_seq = seq // tm
    n_t = t // tm
    vec = lambda: pl.BlockSpec((1, d), lambda i: (0, 0))
    return pl.pallas_call(
        _combine_kernel,
        grid=(n_t,),
        in_specs=[
            pl.BlockSpec((tm, d), lambda i: (i, 0)),
            pl.BlockSpec((tm, d), lambda i: (i, 0)),
            pl.BlockSpec((tm, d), lambda i: (i + n_t, 0)),
            pl.BlockSpec((tm, LANES), lambda i: (i, 0)),
            pl.BlockSpec((None, 1, d), lambda i: (i // per_seq, 0, 0)),
            vec(), vec(),
        ],
        out_specs=pl.BlockSpec((tm, d), lambda i: (i, 0)),
        out_shape=jax.ShapeDtypeStruct((t, d), F32),
        compiler_params=_params(("arbitrary",)),
        name="combine",
    )(x, y, y, gw, g2, lng, lnb)


def _kv_kernel(x_ref, sc_ref, sh_ref, w_ref, ka_ref, va_ref, km_ref, *, blocks_per_seq):
    h = _modulate(x_ref[...], sc_ref[...], sh_ref[...]).astype(BF16)
    kv = jnp.dot(h, w_ref[...], preferred_element_type=F32)
    k = kv[:, :D_MODEL]
    v = kv[:, D_MODEL:]
    km_ref[...] = jnp.mean(k, axis=0, keepdims=True)
    j = pl.program_id(0) % blocks_per_seq
    shape = (MOBA_BLOCK, HEAD_DIM)
    col = lax.broadcasted_iota(jnp.int32, shape, 1)
    pos = lax.broadcasted_iota(jnp.int32, shape, 0).astype(F32)
    blk = (j * MOBA_BLOCK).astype(F32)
    kx = jnp.where(col < 3, pos,
                   jnp.where(col < 6, blk,
                             jnp.where((col >= 8) & (col < 11), 1.0,
                                       jnp.where(col == 16 + j, 1.0, 0.0)))).astype(BF16)
    vx = jnp.where(col == 0, 1.0, 0.0).astype(BF16)
    for hd in range(N_HEADS):
        lo = hd * HEAD_DIM
        ka_ref[hd] = jnp.concatenate([k[:, lo:lo + HEAD_DIM].astype(BF16), kx], axis=1)
        va_ref[hd] = jnp.concatenate([v[:, lo:lo + HEAD_DIM].astype(BF16), vx], axis=1)


def _shared_kv(x, sc, sh, w, batch, seq):
    t, d = x.shape
    tm = MOBA_BLOCK
    n_blk = seq // tm
    mod = lambda: pl.BlockSpec((None, 1, d), lambda i: (i // n_blk, 0, 0))
    aug = lambda: pl.BlockSpec((None, N_HEADS, tm, AUG), lambda i: (i // n_blk, 0, i % n_blk, 0))
    return pl.pallas_call(
        functools.partial(_kv_kernel, blocks_per_seq=n_blk),
        grid=(t // tm,),
        in_specs=[
            pl.BlockSpec((tm, d), lambda i: (i, 0)),
            mod(), mod(),
            pl.BlockSpec((d, 2 * d), lambda i: (0, 0)),
        ],
        out_specs=[aug(), aug(), pl.BlockSpec((None, None, 1, d), lambda i: (i // n_blk, i % n_blk, 0, 0))],
        out_shape=[
            jax.ShapeDtypeStruct((batch, N_HEADS, seq, AUG), BF16),
            jax.ShapeDtypeStruct((batch, N_HEADS, seq, AUG), BF16),
            jax.ShapeDtypeStruct((batch, n_blk, 1, d), F32),
        ],
        compiler_params=_params(("arbitrary",)),
        name="shared_kv",
    )(x, sc, sh, w)


def _split3(x):
    p1 = x.astype(BF16).astype(F32)
    p2 = (x - p1).astype(BF16).astype(F32)
    p3 = (x - p1 - p2).astype(BF16).astype(F32)
    return p1, p2, p3


def _dot_nt(a, b):
    return lax.dot_general(a, b, (((1,), (1,)), ((), ())), preferred_element_type=F32)


def _qaug_kernel(sl_ref, x_ref, sc_ref, sh_ref, w_ref, kmbd_ref, qa_ref, *, n_blk):
    i = pl.program_id(0) % n_blk
    bq = MOBA_BLOCK
    h = _modulate(x_ref[...], sc_ref[...], sh_ref[...]).astype(BF16)
    qb = (jnp.dot(h, w_ref[...], preferred_element_type=F32) * (HEAD_DIM ** -0.5 * LOG2E)).astype(BF16)
    gates = _dot_nt(kmbd_ref[0], qb) + _dot_nt(kmbd_ref[1], qb) + _dot_nt(kmbd_ref[2], qb)
    lane_q = lax.broadcasted_iota(jnp.int32, (SUBLANES, bq), 1).astype(F32)
    sub8 = lax.broadcasted_iota(jnp.int32, (SUBLANES, bq), 0)
    blk_id = lax.broadcasted_iota(jnp.int32, (n_blk, bq), 0)
    t_q = lane_q + (i * bq).astype(F32)
    group = LANES // EXT
    for g0 in range(0, N_HEADS, group):
        exts = []
        for hd in range(g0, g0 + group):
            gate = jnp.where(blk_id < i, gates[hd * n_blk:(hd + 1) * n_blk, :], -jnp.inf)
            rank = jnp.zeros((n_blk, bq), jnp.int32)
            for m in range(n_blk):
                gm = gate[m:m + 1, :]
                ahead = (gm > gate) | ((gm == gate) & (blk_id > m))
                rank = rank + ahead.astype(jnp.int32)
            bias = jnp.where((blk_id < i) & (rank >= MOBA_TOPK), NEG_INF, 0.0)
            slope = jnp.full((SUBLANES, bq), sl_ref[hd], F32) * LOG2E
            s1, s2, s3 = _split3(slope)
            t1, t2, t3 = _split3(-slope * t_q)
            rows_s = jnp.where((sub8 == 0) | (sub8 == 3), s1,
                               jnp.where((sub8 == 1) | (sub8 == 4), s2,
                                         jnp.where((sub8 == 2) | (sub8 == 5), s3, 0.0)))
            rows_t = jnp.where(sub8 == 0, t1, jnp.where(sub8 == 1, t2, jnp.where(sub8 == 2, t3, 0.0)))
            pieces = [rows_s, rows_t, bias]
            if EXT - 2 * SUBLANES - n_blk:
                pieces.append(jnp.zeros((EXT - 2 * SUBLANES - n_blk, bq), F32))
            exts.extend(pieces)
        ext_t = jnp.concatenate(exts, axis=0).T.astype(BF16)
        pad = jnp.zeros((bq, AUG - HEAD_DIM - EXT), BF16)
        for k, hd in enumerate(range(g0, g0 + group)):
            qa_ref[hd] = jnp.concatenate([qb[:, hd * HEAD_DIM:(hd + 1) * HEAD_DIM],
                                          ext_t[:, k * EXT:(k + 1) * EXT], pad], axis=1)


def _q_aug(slopes, x, sc, sh, w, layer, kmbd, batch, seq):
    t, d = x.shape
    tm = MOBA_BLOCK
    n_blk = seq // tm
    rows = kmbd.shape[2]
    mod = lambda: pl.BlockSpec((None, 1, d), lambda i, sl: (i // n_blk, 0, 0))
    grid_spec = pltpu.PrefetchScalarGridSpec(
        num_scalar_prefetch=1,
        grid=(t // tm,),
        in_specs=[
            pl.BlockSpec((tm, d), lambda i, sl: (i, 0)),
            mod(), mod(),
            pl.BlockSpec((None, d, d), lambda i, sl: (layer, 0, 0)),
            pl.BlockSpec((None, 3, rows, d), lambda i, sl: (i // n_blk, 0, 0, 0)),
        ],
        out_specs=pl.BlockSpec((None, N_HEADS, tm, AUG), lambda i, sl: (i // n_blk, 0, i % n_blk, 0)),
    )
    return pl.pallas_call(
        functools.partial(_qaug_kernel, n_blk=n_blk),
        grid_spec=grid_spec,
        out_shape=jax.ShapeDtypeStruct((batch, N_HEADS, seq, AUG), BF16),
        compiler_params=_params(("arbitrary",)),
        name="q_aug",
    )(slopes, x, sc, sh, w, kmbd)


def _block_diag_means(km, n_blk):
    d = km.shape[-1]
    r = jnp.arange(N_HEADS * n_blk)
    head_of_col = jnp.arange(d) // HEAD_DIM
    bd = jnp.where((r // n_blk)[None, :, None] == head_of_col[None, None, :], km[:, r % n_blk, :], 0.0)
    return jnp.stack(_split3(bd), axis=1).astype(BF16)


def _attn_kernel(qlo_ref, qhi_ref, ka_ref, va_ref, olo_ref, ohi_ref, q_scr, s_scr, *, n_blk):
    i = pl.program_id(2)
    bq = MOBA_BLOCK
    half = n_blk // 2
    q_scr[0] = qlo_ref[...]
    q_scr[1] = qhi_ref[...]
    dmat = lax.broadcasted_iota(jnp.int32, (bq, bq), 1) - lax.broadcasted_iota(jnp.int32, (bq, bq), 0)

    def unit(t):
        if t >= half:
            return None, 1, pl.multiple_of((t - i - 1) * bq, bq)
        is_lo = i >= t
        key = jnp.where(is_lo, t, t - i - 1)
        return is_lo, jnp.where(is_lo, 0, 1), pl.multiple_of(key * bq, bq)

    neg = jnp.full((bq, LANES), NEG_INF, F32)
    mx = [[neg] * HEAD_PAIR, [neg] * HEAD_PAIR]
    for t in range(n_blk + 1):
        is_lo, sel, start = unit(t)
        if is_lo is not None:
            causal = dmat <= jnp.where(is_lo, i - t, n_blk - t) * bq
        elif t == n_blk:
            causal = dmat <= 0
        else:
            causal = None
        for hh in range(HEAD_PAIR):
            s = _dot_nt(q_scr[sel, hh], ka_ref[hh, pl.ds(start, bq), :])
            if causal is not None:
                s = jnp.where(causal, s, NEG_INF)
            s_scr[hh, t] = s
            smax = jnp.maximum(s[:, :LANES], s[:, LANES:])
            if is_lo is None:
                mx[1][hh] = jnp.maximum(mx[1][hh], smax)
            else:
                mx[0][hh] = jnp.maximum(mx[0][hh], jnp.where(is_lo, smax, NEG_INF))
                mx[1][hh] = jnp.maximum(mx[1][hh], jnp.where(is_lo, NEG_INF, smax))

    m_rows = [[jnp.broadcast_to(jnp.max(mx[sel][hh], axis=1, keepdims=True), (bq, LANES))
               for hh in range(HEAD_PAIR)] for sel in range(2)]
    zero = jnp.zeros((bq, AUG), F32)
    acc = [[zero] * HEAD_PAIR, [zero] * HEAD_PAIR]
    for t in range(n_blk + 1):
        is_lo, _, start = unit(t)
        for hh in range(HEAD_PAIR):
            m_row = m_rows[1][hh] if is_lo is None else jnp.where(is_lo, m_rows[0][hh], m_rows[1][hh])
            p = jnp.exp2(s_scr[hh, t] - jnp.concatenate([m_row, m_row], axis=1)).astype(BF16)
            pv = jnp.dot(p, va_ref[hh, pl.ds(start, bq), :], preferred_element_type=F32)
            if is_lo is None:
                acc[1][hh] = acc[1][hh] + pv
            else:
                acc[0][hh] = acc[0][hh] + jnp.where(is_lo, pv, 0.0)
                acc[1][hh] = acc[1][hh] + jnp.where(is_lo, 0.0, pv)

    for sel, o_ref in enumerate((olo_ref, ohi_ref)):
        outs = [a[:, :HEAD_DIM] / a[:, HEAD_DIM:HEAD_DIM + 1] for a in acc[sel]]
        o_ref[...] = jnp.concatenate(outs, axis=1).astype(o_ref.dtype)


def _attention(qa, ka, va, batch, seq):
    n_blk = seq // MOBA_BLOCK
    assert n_blk % 2 == 0
    pair = lambda: pl.BlockSpec((None, HEAD_PAIR, seq, AUG), lambda b, hp, i: (b, hp, 0, 0))
    half = n_blk // 2
    o_shape = jax.ShapeDtypeStruct((batch, seq // 2, D_MODEL), BF16)
    out_lo, out_hi = pl.pallas_call(
        functools.partial(_attn_kernel, n_blk=n_blk),
        grid=(batch, N_HEADS // HEAD_PAIR, n_blk // 2),
        in_specs=[
            pl.BlockSpec((None, HEAD_PAIR, MOBA_BLOCK, AUG), lambda b, hp, i: (b, hp, i, 0)),
            pl.BlockSpec((None, HEAD_PAIR, MOBA_BLOCK, AUG), lambda b, hp, i: (b, hp, n_blk - 1 - i, 0)),
            pair(), pair(),
        ],
        out_specs=[
            pl.BlockSpec((None, MOBA_BLOCK, HEAD_PAIR * HEAD_DIM), lambda b, hp, i: (b, i, hp)),
            pl.BlockSpec((None, MOBA_BLOCK, HEAD_PAIR * HEAD_DIM), lambda b, hp, i: (b, half - 1 - i, hp)),
        ],
        out_shape=[o_shape, o_shape],
        scratch_shapes=[
            pltpu.VMEM((2, HEAD_PAIR, MOBA_BLOCK, AUG), BF16),
            pltpu.VMEM((HEAD_PAIR, n_blk + 1, MOBA_BLOCK, MOBA_BLOCK), F32),
        ],
        compiler_params=_params(("arbitrary", "arbitrary", "arbitrary")),
        name="moba_attn",
    )(qa, qa, ka, va)
    return jnp.concatenate([out_lo, out_hi], axis=1)


def kernel(x, c, ada_w, ada_b, ln_g, ln_b, conv_in_w, conv_in_b, conv_dw_w, conv_dw_b, conv_norm_g,
           conv_norm_b, conv_out_w, conv_out_b, kv_ada_w, kv_ada_b, w_kv, w_q, w_o, ffn_w13, ffn_w2,
           router_w, router_b, moe_w13, moe_w2):
    batch, seq, d = x.shape
    t = batch * seq
    assert d == D_MODEL and seq % ROW_TILE == 0 and seq % MOBA_BLOCK == 0 and t & (t - 1) == 0
    n_blk = seq // MOBA_BLOCK
    assert n_blk <= 16

    c_pad = jnp.pad(c, ((0, SUBLANES - batch), (0, 0)))
    mods = _ada(c_pad, ada_w, ada_b, 1536)[:, :batch]
    kv_mod = _ada(c_pad, kv_ada_w[None], kv_ada_b[None], 1024)[0, :batch]

    def mod_vec(v):
        return v.reshape(batch, 1, d)

    row = lambda v: v.reshape(1, -1)
    slopes = jnp.exp2(-8.0 * jnp.arange(1, N_HEADS + 1, dtype=F32) / N_HEADS)
    n_moe_tiles = 2 * t // MOE_TILE + N_EXPERTS
    conv_in_wb, conv_out_wb, w_qb, w_ob = [w.astype(BF16) for w in (conv_in_w, conv_out_w, w_q, w_o)]
    ffn_w13b, ffn_w2b, moe_w13b, moe_w2b = [w.astype(BF16) for w in (ffn_w13, ffn_w2, moe_w13, moe_w2)]

    xf = x.reshape(t, d)
    ka = va = kmbd = None
    for l in range(DEPTH):
        if l == N_A_LAYERS:
            ka, va, km = _shared_kv(xf, mod_vec(kv_mod[:, d:]), mod_vec(kv_mod[:, :d]),
                                    w_kv.astype(BF16), batch, seq)
            kmbd = _block_diag_means(km.reshape(batch, n_blk, d), n_blk)
        sh1, sc1, g1, sh2, sc2, g2 = [mod_vec(mods[l, :, k * d:(k + 1) * d]) for k in range(6)]
        lng1, lnb1 = row(ln_g[l, 0]), row(ln_b[l, 0])
        lng2, lnb2 = row(ln_g[l, 1]), row(ln_b[l, 1])
        if l < N_A_LAYERS:
            u = _conv_in(xf, sc1, sh1, conv_in_wb, l, row(conv_in_b[l]), seq)
            xf = _conv_block(u, xf, g1, conv_dw_w[l], row(conv_dw_b[l]), row(conv_norm_g[l]),
                             row(conv_norm_b[l]), conv_out_wb, l, row(conv_out_b[l]), lng1, lnb1, seq)
        else:
            jl = l - N_A_LAYERS
            qa = _q_aug(slopes, xf, sc1, sh1, w_qb, jl, kmbd, batch, seq)
            att = _attention(qa, ka, va, batch, seq)
            xf = _oproj(att.reshape(t, d), xf, g1, w_ob, jl, lng1, lnb1, seq)
        if l % 2 == 0:
            xf = _ffn(xf, sc2, sh2, g2, ffn_w13b, ffn_w2b, l // 2, lng2, lnb2, seq)
        else:
            e = l // 2
            h, meta_rows, meta_cols = _router(xf, sc2, sh2, router_w[e].T, router_b[e].reshape(-1, 1), seq)
            te, nvalid, row_dst = _route_rows(meta_rows[0].astype(jnp.int32), meta_rows[1].astype(jnp.int32),
                                              MOE_TILE, n_moe_tiles)
            y = _moe(h, te, nvalid, row_dst, moe_w13b, moe_w2b, e)
            xf = _combine(xf, y, meta_cols, g2, lng2, lnb2, seq)
    return xf.reshape(batch, seq, d)
```

```python
import functools

import jax
import jax.numpy as jnp
from jax import lax
from jax.experimental import pallas as pl
from jax.experimental.pallas import tpu as pltpu

F32 = jnp.float32
BF16 = jnp.bfloat16
HIGHEST = lax.Precision.HIGHEST

D_MODEL = 1024
DEPTH = 4
N_A_LAYERS = DEPTH // 2
CONV_K = 31
N_HEADS = 16
HEAD_DIM = D_MODEL // N_HEADS
MOBA_BLOCK = 256
MOBA_TOPK = 3
D_FF = 2816
N_EXPERTS = 8
D_FF_EXPERT = 3584
ALPHA = (2.0 * DEPTH) ** 0.25
LN_EPS = 1e-5
NEG_INF = -1e30

SUBLANES = 8
LANES = 128
VMEM_LIMIT = 48 * 1024 * 1024

ROW_TILE = 512
CONV_HALO = 32
CONV_TILE = 256
CONV_CHUNK = 32
FFN_CHUNK = 1408
MOE_CHUNK = 1792
MOE_TILE = 512
HEAD_PAIR = 2
AUG = 2 * HEAD_DIM
EXT = 32
LOG2E = 1.4426950408889634


def _params(sem, vmem=VMEM_LIMIT):
    return pltpu.CompilerParams(dimension_semantics=sem, vmem_limit_bytes=vmem)


def _layer_norm(y, g, b):
    mu = jnp.mean(y, axis=-1, keepdims=True)
    yc = y - mu
    var = jnp.mean(yc * yc, axis=-1, keepdims=True)
    return yc * lax.rsqrt(var + LN_EPS) * g + b


def _modulate(x, sc, sh):
    return x * (1.0 + sc) + sh


def _ada_kernel(c_ref, w_ref, b_ref, o_ref):
    ca = jax.nn.silu(c_ref[...])
    o_ref[...] = jnp.dot(ca, w_ref[...], precision=HIGHEST, preferred_element_type=F32) + b_ref[...]


def _ada(c_pad, w, b, tn):
    n_l, d, n = w.shape
    rows = c_pad.shape[0]
    return pl.pallas_call(
        _ada_kernel,
        grid=(n_l, n // tn),
        in_specs=[
            pl.BlockSpec((rows, d), lambda l, j: (0, 0)),
            pl.BlockSpec((None, d, tn), lambda l, j: (l, 0, j)),
            pl.BlockSpec((None, 1, tn), lambda l, j: (l, 0, j)),
        ],
        out_specs=pl.BlockSpec((None, rows, tn), lambda l, j: (l, 0, j)),
        out_shape=jax.ShapeDtypeStruct((n_l, rows, n), F32),
        compiler_params=_params(("arbitrary", "arbitrary")),
        name="ada",
    )(c_pad, w, b.reshape(n_l, 1, n))


def _conv_in_kernel(x_ref, sc_ref, sh_ref, w_ref, b_ref, o_ref):
    h = _modulate(x_ref[...], sc_ref[...], sh_ref[...]).astype(BF16)
    y = jnp.dot(h, w_ref[...], preferred_element_type=F32) + b_ref[...]
    half = y.shape[1] // 2
    o_ref[...] = y[:, :half] * jax.nn.sigmoid(y[:, half:])


def _conv_in(x, sc, sh, w, layer, b, seq):
    t, d = x.shape
    n = w.shape[2]
    n_out = n // 2
    tm = ROW_TILE
    per_seq = seq // tm
    return pl.pallas_call(
        _conv_in_kernel,
        grid=(t // tm,),
        in_specs=[
            pl.BlockSpec((tm, d), lambda i: (i, 0)),
            pl.BlockSpec((None, 1, d), lambda i: (i // per_seq, 0, 0)),
            pl.BlockSpec((None, 1, d), lambda i: (i // per_seq, 0, 0)),
            pl.BlockSpec((None, d, n), lambda i: (layer, 0, 0)),
            pl.BlockSpec((1, n), lambda i: (0, 0)),
        ],
        out_specs=pl.BlockSpec((tm, n_out), lambda i: (i, 0)),
        out_shape=jax.ShapeDtypeStruct((t, n_out), F32),
        compiler_params=_params(("arbitrary",)),
        name="conv_in",
    )(x, sc, sh, w, b)


def _conv_kernel(uprev_ref, u_ref, x_ref, g1_ref, wdw_ref, bdw_ref, gn_ref, bn_ref, wo_ref, bo_ref,
                 lng_ref, lnb_ref, o_ref, win_ref, cv_ref, *, per_seq):
    tm = u_ref.shape[0]
    rows = CONV_HALO + tm
    first = (pl.program_id(0) % per_seq) == 0
    win_ref[0, 0:CONV_HALO, :] = jnp.where(first, 0.0, uprev_ref[...])
    win_ref[0, CONV_HALO:, :] = u_ref[...]
    groups = rows // SUBLANES - 1
    cur = win_ref[0, SUBLANES:, :].reshape(groups, SUBLANES, D_MODEL)
    prev = win_ref[0, 0:rows - SUBLANES, :].reshape(groups, SUBLANES, D_MODEL)
    sub = lax.broadcasted_iota(jnp.int32, (groups, SUBLANES, D_MODEL), 1)
    for b in range(1, SUBLANES):
        mixed = jnp.where(sub < SUBLANES - b, cur, prev)
        win_ref[b, SUBLANES:, :] = pltpu.roll(mixed, b, axis=1).reshape(rows - SUBLANES, D_MODEL)
    for c in range(tm // CONV_CHUNK):
        base = CONV_HALO + c * CONV_CHUNK
        acc = jnp.zeros((CONV_CHUNK, D_MODEL), F32) + bdw_ref[...]
        for s in range(CONV_K):
            a, b = divmod(s, SUBLANES)
            k = CONV_K - 1 - s
            acc = acc + win_ref[b, base - SUBLANES * a:base - SUBLANES * a + CONV_CHUNK, :] * wdw_ref[k:k + 1, :]
        cv_ref[c * CONV_CHUNK:(c + 1) * CONV_CHUNK, :] = acc
    un = jax.nn.silu(_layer_norm(cv_ref[...], gn_ref[...], bn_ref[...])).astype(BF16)
    out = jnp.dot(un, wo_ref[...], preferred_element_type=F32) + bo_ref[...]
    y = ALPHA * x_ref[...] + (1.0 + g1_ref[...]) * out
    o_ref[...] = _layer_norm(y, lng_ref[...], lnb_ref[...])


def _conv_block(u, x, g1, wdw, bdw, gn, bn, wo, layer, bo, lng, lnb, seq):
    t, d = x.shape
    tm = CONV_TILE
    per_seq = seq // tm
    halo_per_tile = tm // CONV_HALO
    vec = lambda: pl.BlockSpec((1, d), lambda i: (0, 0))
    return pl.pallas_call(
        functools.partial(_conv_kernel, per_seq=per_seq),
        grid=(t // tm,),
        in_specs=[
            pl.BlockSpec((CONV_HALO, d), lambda i: (jnp.maximum(i * halo_per_tile - 1, 0), 0)),
            pl.BlockSpec((tm, d), lambda i: (i, 0)),
            pl.BlockSpec((tm, d), lambda i: (i, 0)),
            pl.BlockSpec((None, 1, d), lambda i: (i // per_seq, 0, 0)),
            pl.BlockSpec((CONV_K, d), lambda i: (0, 0)),
            vec(), vec(), vec(),
            pl.BlockSpec((None, d, d), lambda i: (layer, 0, 0)),
            vec(), vec(), vec(),
        ],
        out_specs=pl.BlockSpec((tm, d), lambda i: (i, 0)),
        out_shape=jax.ShapeDtypeStruct((t, d), F32),
        scratch_shapes=[pltpu.VMEM((SUBLANES, CONV_HALO + tm, d), F32), pltpu.VMEM((tm, d), F32)],
        compiler_params=_params(("arbitrary",)),
        name="conv_block",
    )(u, u, x, g1, wdw, bdw, gn, bn, wo, bo, lng, lnb)


def _oproj_kernel(alo_ref, ahi_ref, x_ref, g1_ref, w_ref, lng_ref, lnb_ref, o_ref, *, per_seq):
    first_half = (pl.program_id(0) % per_seq) < per_seq // 2
    a = jnp.where(first_half, alo_ref[...], ahi_ref[...])
    out = jnp.dot(a, w_ref[...], preferred_element_type=F32)
    y = ALPHA * x_ref[...] + (1.0 + g1_ref[...]) * out
    o_ref[...] = _layer_norm(y, lng_ref[...], lnb_ref[...])


def _oproj(a_lo, a_hi, x, g1, w, layer, lng, lnb, seq):
    t, d = x.shape
    tm = ROW_TILE
    per_seq = seq // tm
    half = per_seq // 2
    vec = lambda: pl.BlockSpec((1, d), lambda i: (0, 0))
    return pl.pallas_call(
        functools.partial(_oproj_kernel, per_seq=per_seq),
        grid=(t // tm,),
        in_specs=[
            pl.BlockSpec((None, tm, d), lambda i: (i // per_seq, jnp.minimum(i % per_seq, half - 1), 0)),
            pl.BlockSpec((None, tm, d), lambda i: (i // per_seq, jnp.maximum(i % per_seq - half, 0), 0)),
            pl.BlockSpec((tm, d), lambda i: (i, 0)),
            pl.BlockSpec((None, 1, d), lambda i: (i // per_seq, 0, 0)),
            pl.BlockSpec((None, d, d), lambda i: (layer, 0, 0)),
            vec(), vec(),
        ],
        out_specs=pl.BlockSpec((tm, d), lambda i: (i, 0)),
        out_shape=jax.ShapeDtypeStruct((t, d), F32),
        compiler_params=_params(("arbitrary",)),
        name="oproj",
    )(a_lo, a_hi, x, g1, w, lng, lnb)


def _swiglu_chunk(hb, wa_ref, wb_ref, w2_ref):
    a = jnp.dot(hb, wa_ref[...], preferred_element_type=F32)
    b = jnp.dot(hb, wb_ref[...], preferred_element_type=F32)
    g = (jax.nn.silu(a) * b).astype(BF16)
    return jnp.dot(g, w2_ref[...], preferred_element_type=F32)


def _ffn_kernel(x_ref, sc_ref, sh_ref, g2_ref, wa_ref, wb_ref, w2_ref, lng_ref, lnb_ref, o_ref,
                hb_ref, acc_ref):
    j = pl.program_id(1)

    @pl.when(j == 0)
    def _():
        hb_ref[...] = _modulate(x_ref[...], sc_ref[...], sh_ref[...]).astype(BF16)
        acc_ref[...] = jnp.zeros_like(acc_ref)

    acc_ref[...] += _swiglu_chunk(hb_ref[...], wa_ref, wb_ref, w2_ref)

    @pl.when(j == pl.num_programs(1) - 1)
    def _():
        y = ALPHA * x_ref[...] + (1.0 + g2_ref[...]) * acc_ref[...]
        o_ref[...] = _layer_norm(y, lng_ref[...], lnb_ref[...])


def _ffn(x, sc, sh, g2, w13, w2, layer, lng, lnb, seq):
    t, d = x.shape
    f = w2.shape[1]
    tm, fc = ROW_TILE, FFN_CHUNK
    n_fc = f // fc
    per_seq = seq // tm
    mod = lambda: pl.BlockSpec((None, 1, d), lambda i, j: (i // per_seq, 0, 0))
    vec = lambda: pl.BlockSpec((1, d), lambda i, j: (0, 0))
    return pl.pallas_call(
        _ffn_kernel,
        grid=(t // tm, n_fc),
        in_specs=[
            pl.BlockSpec((tm, d), lambda i, j: (i, 0)),
            mod(), mod(), mod(),
            pl.BlockSpec((None, d, fc), lambda i, j: (layer, 0, j)),
            pl.BlockSpec((None, d, fc), lambda i, j: (layer, 0, j + n_fc)),
            pl.BlockSpec((None, fc, d), lambda i, j: (layer, j, 0)),
            vec(), vec(),
        ],
        out_specs=pl.BlockSpec((tm, d), lambda i, j: (i, 0)),
        out_shape=jax.ShapeDtypeStruct((t, d), F32),
        scratch_shapes=[pltpu.VMEM((tm, d), BF16), pltpu.VMEM((tm, d), F32)],
        compiler_params=_params(("arbitrary", "arbitrary")),
        name="ffn",
    )(x, sc, sh, g2, w13, w13, w2, lng, lnb)


def _router_kernel(x_ref, sc_ref, sh_ref, rwt_ref, rb_ref, h_ref, rows_ref, cols_ref):
    h = _modulate(x_ref[...], sc_ref[...], sh_ref[...])
    h_ref[...] = h
    lg = lax.dot_general(rwt_ref[...], h, (((1,), (1,)), ((), ())), precision=HIGHEST,
                         preferred_element_type=F32) + rb_ref[...]
    idx = lax.broadcasted_iota(jnp.int32, lg.shape, 0)
    m1 = jnp.max(lg, axis=0, keepdims=True)
    i1 = jnp.min(jnp.where(lg == m1, idx, N_EXPERTS), axis=0, keepdims=True)
    lg2 = jnp.where(idx == i1, -jnp.inf, lg)
    m2 = jnp.max(lg2, axis=0, keepdims=True)
    i2 = jnp.min(jnp.where(lg2 == m2, idx, N_EXPERTS), axis=0, keepdims=True)
    e2 = jnp.exp(m2 - m1)
    den = 1.0 + e2
    w1 = 1.0 / den
    w2 = e2 / den
    meta = jnp.where(idx == 0, i1.astype(F32),
                     jnp.where(idx == 1, i2.astype(F32),
                               jnp.where(idx == 2, w1, jnp.where(idx == 3, w2, 0.0))))
    rows_ref[...] = meta
    wide = jnp.concatenate([meta, jnp.zeros((LANES - N_EXPERTS, meta.shape[1]), F32)], axis=0)
    cols_ref[...] = wide.T


def _router(x, sc, sh, rwt, rb, seq):
    t, d = x.shape
    tm = ROW_TILE
    per_seq = seq // tm
    mod = lambda: pl.BlockSpec((None, 1, d), lambda i: (i // per_seq, 0, 0))
    return pl.pallas_call(
        _router_kernel,
        grid=(t // tm,),
        in_specs=[
            pl.BlockSpec((tm, d), lambda i: (i, 0)),
            mod(), mod(),
            pl.BlockSpec((N_EXPERTS, d), lambda i: (0, 0)),
            pl.BlockSpec((N_EXPERTS, 1), lambda i: (0, 0)),
        ],
        out_specs=[
            pl.BlockSpec((tm, d), lambda i: (i, 0)),
            pl.BlockSpec((N_EXPERTS, tm), lambda i: (0, i)),
            pl.BlockSpec((tm, LANES), lambda i: (i, 0)),
        ],
        out_shape=[
            jax.ShapeDtypeStruct((t, d), F32),
            jax.ShapeDtypeStruct((N_EXPERTS, t), F32),
            jax.ShapeDtypeStruct((t, LANES), F32),
        ],
        compiler_params=_params(("arbitrary",)),
        name="router",
    )(x, sc, sh, rwt, rb)


def _moe_kernel(te_ref, nv_ref, tcur_ref, tnext_ref, dprev_ref, h_hbm, wa_ref, wb_ref, w2_ref, y_hbm,
                hbuf, hb_ref, obuf, gsem, ssem, *, n_fc):
    i = pl.program_id(0)
    j = pl.program_id(1)
    n_tiles = pl.num_programs(0)
    _, rows_per_step, d = hbuf.shape
    tm = n_fc * rows_per_step
    slot = i % 2
    prev_slot = 1 - slot

    def gather_copy(tok, step, r):
        return pltpu.make_async_copy(h_hbm.at[pl.ds(tok, 1), :], hbuf.at[step, pl.ds(r, 1), :], gsem.at[0])

    def scatter_copy(step, r, dst):
        return pltpu.make_async_copy(obuf.at[prev_slot, step, pl.ds(r, 1), :], y_hbm.at[pl.ds(dst, 1), :],
                                     ssem.at[0])

    def wait_gather():
        pltpu.make_async_copy(hbuf, hbuf, gsem.at[0]).wait()

    def wait_scatter():
        pltpu.make_async_copy(obuf.at[0], obuf.at[0], ssem.at[0]).wait()

    @pl.when(j == 0)
    def _():
        @pl.when(i == 0)
        def _():
            obuf[...] = jnp.zeros_like(obuf)

            def body(row, carry):
                gather_copy(tcur_ref[0, row], row // rows_per_step, row % rows_per_step).start()
                return carry
            lax.fori_loop(0, tm, body, 0)

        wait_gather()
        hb_ref[...] = hbuf[...].reshape(tm, d).astype(BF16)

        @pl.when(i >= 1)
        def _():
            wait_scatter()

    def issue_row_dmas():
        base = j * rows_per_step
        for r in range(rows_per_step):
            gather_copy(tnext_ref[0, base + r], j, r).start()
            scatter_copy(j, r, dprev_ref[0, base + r]).start()

    used = nv_ref[i] > 0

    @pl.when(used)
    def _():
        contrib = _swiglu_chunk(hb_ref[...], wa_ref, wb_ref, w2_ref).reshape(n_fc, rows_per_step, d)
        issue_row_dmas()

        @pl.when(j == 0)
        def _():
            obuf[slot] = contrib

        @pl.when(j > 0)
        def _():
            obuf[slot] += contrib

    @pl.when(jnp.logical_not(used))
    def _():
        issue_row_dmas()

    @pl.when((j == n_fc - 1) & (i == n_tiles - 1))
    def _():
        wait_gather()
        wait_scatter()


def _moe(h, tile_expert, tile_nvalid, row_dst, w13, w2, layer):
    t, d = h.shape
    n_tiles, _, tm = row_dst.shape
    f = w2.shape[2]
    fc = MOE_CHUNK
    n_fc = f // fc
    lead = (2 * t + tm + jnp.arange(tm, dtype=jnp.int32)).reshape(1, 1, tm)
    prev_dst = jnp.concatenate([lead, row_dst[:-1]], axis=0)
    row_tok = row_dst & (t - 1)

    def chunk(i, j, nv):
        return jnp.where(nv[i] > 0, j, n_fc - 1)

    grid_spec = pltpu.PrefetchScalarGridSpec(
        num_scalar_prefetch=2,
        grid=(n_tiles, n_fc),
        in_specs=[
            pl.BlockSpec((None, 1, tm), lambda i, j, te, nv: (i, 0, 0), memory_space=pltpu.SMEM),
            pl.BlockSpec((None, 1, tm), lambda i, j, te, nv: (jnp.minimum(i + 1, n_tiles - 1), 0, 0),
                         memory_space=pltpu.SMEM),
            pl.BlockSpec((None, 1, tm), lambda i, j, te, nv: (i, 0, 0), memory_space=pltpu.SMEM),
            pl.BlockSpec(memory_space=pl.ANY),
            pl.BlockSpec((None, None, d, fc), lambda i, j, te, nv: (layer, te[i], 0, chunk(i, j, nv))),
            pl.BlockSpec((None, None, d, fc), lambda i, j, te, nv: (layer, te[i], 0, chunk(i, j, nv) + n_fc)),
            pl.BlockSpec((None, None, fc, d), lambda i, j, te, nv: (layer, te[i], chunk(i, j, nv), 0)),
        ],
        out_specs=pl.BlockSpec(memory_space=pl.ANY),
        scratch_shapes=[
            pltpu.VMEM((n_fc, tm // n_fc, d), F32),
            pltpu.VMEM((tm, d), BF16),
            pltpu.VMEM((2, n_fc, tm // n_fc, d), F32),
            pltpu.SemaphoreType.DMA((1,)),
            pltpu.SemaphoreType.DMA((1,)),
        ],
    )
    return pl.pallas_call(
        functools.partial(_moe_kernel, n_fc=n_fc),
        grid_spec=grid_spec,
        out_shape=jax.ShapeDtypeStruct((2 * t + 2 * tm, d), F32),
        compiler_params=_params(("arbitrary", "arbitrary")),
        name="moe",
    )(tile_expert, tile_nvalid, row_tok, row_tok, prev_dst, h, w13, w13, w2)


def _route_rows(e1, e2, tm, n_tiles):
    t = e1.shape[0]
    experts = jnp.concatenate([e1, e2])
    order = jnp.argsort(experts, stable=True).astype(jnp.int32)
    counts = jnp.sum(experts[:, None] == jnp.arange(N_EXPERTS)[None, :], axis=0).astype(jnp.int32)
    tiles_per = (counts + tm - 1) // tm
    tile_end = jnp.cumsum(tiles_per)
    tile_start = tile_end - tiles_per
    group_start = jnp.cumsum(counts) - counts
    tile_ids = jnp.arange(n_tiles, dtype=jnp.int32)
    te = jnp.minimum(jnp.sum(tile_ids[:, None] >= tile_end[None, :], axis=1), N_EXPERTS - 1).astype(jnp.int32)
    used = tile_ids < tile_end[-1]
    first_row = (tile_ids - tile_start[te]) * tm
    nvalid = jnp.where(used, jnp.clip(counts[te] - first_row, 0, tm), 0).astype(jnp.int32)
    last_e = te[jnp.maximum(tile_end[-1] - 1, 0)]
    te = jnp.where(used, te, last_e)
    r = jnp.arange(tm, dtype=jnp.int32)
    src = group_start[te][:, None] + first_row[:, None] + r[None, :]
    valid = r[None, :] < nvalid[:, None]
    dump = 2 * t + (tile_ids % 2)[:, None] * tm + r[None, :]
    row_dst = jnp.where(valid, order[jnp.clip(src, 0, 2 * t - 1)], dump).astype(jnp.int32)
    return te, nvalid, row_dst.reshape(n_tiles, 1, tm)


def _combine_kernel(x_ref, y0_ref, y1_ref, gw_ref, g2_ref, lng_ref, lnb_ref, o_ref):
    gw = gw_ref[...]
    f = gw[:, 2:3] * y0_ref[...] + gw[:, 3:4] * y1_ref[...]
    y = ALPHA * x_ref[...] + (1.0 + g2_ref[...]) * f
    o_ref[...] = _layer_norm(y, lng_ref[...], lnb_ref[...])


def _combine(x, y, gw, g2, lng, lnb, seq):
    t, d = x.shape
    tm = ROW_TILE
    per_seq = seq // tm
    n_t = t // tm
    vec = lambda: pl.BlockSpec((1, d), lambda i: (0, 0))
    return pl.pallas_call(
        _combine_kernel,
        grid=(n_t,),
        in_specs=[
            pl.BlockSpec((tm, d), lambda i: (i, 0)),
            pl.BlockSpec((tm, d), lambda i: (i, 0)),
            pl.BlockSpec((tm, d), lambda i: (i + n_t, 0)),
            pl.BlockSpec((tm, LANES), lambda i: (i, 0)),
            pl.BlockSpec((None, 1, d), lambda i: (i // per_seq, 0, 0)),
            vec(), vec(),
        ],
        out_specs=pl.BlockSpec((tm, d), lambda i: (i, 0)),
        out_shape=jax.ShapeDtypeStruct((t, d), F32),
        compiler_params=_params(("arbitrary",)),
        name="combine",
    )(x, y, y, gw, g2, lng, lnb)


def _kv_kernel(x_ref, sc_ref, sh_ref, w_ref, ka_ref, va_ref, km_ref, *, blocks_per_seq):
    h = _modulate(x_ref[...], sc_ref[...], sh_ref[...]).astype(BF16)
    kv = jnp.dot(h, w_ref[...], preferred_element_type=F32)
    k = kv[:, :D_MODEL]
    v = kv[:, D_MODEL:]
    km_ref[...] = jnp.mean(k, axis=0, keepdims=True)
    j = pl.program_id(0) % blocks_per_seq
    shape = (MOBA_BLOCK, HEAD_DIM)
    col = lax.broadcasted_iota(jnp.int32, shape, 1)
    pos = lax.broadcasted_iota(jnp.int32, shape, 0).astype(F32)
    blk = (j * MOBA_BLOCK).astype(F32)
    kx = jnp.where(col < 3, pos,
                   jnp.where(col < 6, blk,
                             jnp.where((col >= 8) & (col < 11), 1.0,
                                       jnp.where(col == 16 + j, 1.0, 0.0)))).astype(BF16)
    vx = jnp.where(col == 0, 1.0, 0.0).astype(BF16)
    for hd in range(N_HEADS):
        lo = hd * HEAD_DIM
        ka_ref[hd] = jnp.concatenate([k[:, lo:lo + HEAD_DIM].astype(BF16), kx], axis=1)
        va_ref[hd] = jnp.concatenate([v[:, lo:lo + HEAD_DIM].astype(BF16), vx], axis=1)


def _shared_kv(x, sc, sh, w, batch, seq):
    t, d = x.shape
    tm = MOBA_BLOCK
    n_blk = seq // tm
    mod = lambda: pl.BlockSpec((None, 1, d), lambda i: (i // n_blk, 0, 0))
    aug = lambda: pl.BlockSpec((None, N_HEADS, tm, AUG), lambda i: (i // n_blk, 0, i % n_blk, 0))
    return pl.pallas_call(
        functools.partial(_kv_kernel, blocks_per_seq=n_blk),
        grid=(t // tm,),
        in_specs=[
            pl.BlockSpec((tm, d), lambda i: (i, 0)),
            mod(), mod(),
            pl.BlockSpec((d, 2 * d), lambda i: (0, 0)),
        ],
        out_specs=[aug(), aug(), pl.BlockSpec((None, None, 1, d), lambda i: (i // n_blk, i % n_blk, 0, 0))],
        out_shape=[
            jax.ShapeDtypeStruct((batch, N_HEADS, seq, AUG), BF16),
            jax.ShapeDtypeStruct((batch, N_HEADS, seq, AUG), BF16),
            jax.ShapeDtypeStruct((batch, n_blk, 1, d), F32),
        ],
        compiler_params=_params(("arbitrary",)),
        name="shared_kv",
    )(x, sc, sh, w)


def _split3(x):
    p1 = x.astype(BF16).astype(F32)
    p2 = (x - p1).astype(BF16).astype(F32)
    p3 = (x - p1 - p2).astype(BF16).astype(F32)
    return p1, p2, p3


def _dot_nt(a, b):
    return lax.dot_general(a, b, (((1,), (1,)), ((), ())), preferred_element_type=F32)


def _qaug_kernel(sl_ref, x_ref, sc_ref, sh_ref, w_ref, kmbd_ref, qa_ref, *, n_blk):
    i = pl.program_id(0) % n_blk
    bq = MOBA_BLOCK
    h = _modulate(x_ref[...], sc_ref[...], sh_ref[...]).astype(BF16)
    qb = (jnp.dot(h, w_ref[...], preferred_element_type=F32) * (HEAD_DIM ** -0.5 * LOG2E)).astype(BF16)
    gates = _dot_nt(kmbd_ref[0], qb) + _dot_nt(kmbd_ref[1], qb) + _dot_nt(kmbd_ref[2], qb)
    lane_q = lax.broadcasted_iota(jnp.int32, (SUBLANES, bq), 1).astype(F32)
    sub8 = lax.broadcasted_iota(jnp.int32, (SUBLANES, bq), 0)
    blk_id = lax.broadcasted_iota(jnp.int32, (n_blk, bq), 0)
    t_q = lane_q + (i * bq).astype(F32)
    group = LANES // EXT
    for g0 in range(0, N_HEADS, group):
        exts = []
        for hd in range(g0, g0 + group):
            gate = jnp.where(blk_id < i, gates[hd * n_blk:(hd + 1) * n_blk, :], -jnp.inf)
            rank = jnp.zeros((n_blk, bq), jnp.int32)
            for m in range(n_blk):
                gm = gate[m:m + 1, :]
                ahead = (gm > gate) | ((gm == gate) & (blk_id > m))
                rank = rank + ahead.astype(jnp.int32)
            bias = jnp.where((blk_id < i) & (rank >= MOBA_TOPK), NEG_INF, 0.0)
            slope = jnp.full((SUBLANES, bq), sl_ref[hd], F32) * LOG2E
            s1, s2, s3 = _split3(slope)
            t1, t2, t3 = _split3(-slope * t_q)
            rows_s = jnp.where((sub8 == 0) | (sub8 == 3), s1,
                               jnp.where((sub8 == 1) | (sub8 == 4), s2,
                                         jnp.where((sub8 == 2) | (sub8 == 5), s3, 0.0)))
            rows_t = jnp.where(sub8 == 0, t1, jnp.where(sub8 == 1, t2, jnp.where(sub8 == 2, t3, 0.0)))
            pieces = [rows_s, rows_t, bias]
            if EXT - 2 * SUBLANES - n_blk:
                pieces.append(jnp.zeros((EXT - 2 * SUBLANES - n_blk, bq), F32))
            exts.extend(pieces)
        ext_t = jnp.concatenate(exts, axis=0).T.astype(BF16)
        pad = jnp.zeros((bq, AUG - HEAD_DIM - EXT), BF16)
        for k, hd in enumerate(range(g0, g0 + group)):
            qa_ref[hd] = jnp.concatenate([qb[:, hd * HEAD_DIM:(hd + 1) * HEAD_DIM],
                                          ext_t[:, k * EXT:(k + 1) * EXT], pad], axis=1)


def _q_aug(slopes, x, sc, sh, w, layer, kmbd, batch, seq):
    t, d = x.shape
    tm = MOBA_BLOCK
    n_blk = seq // tm
    rows = kmbd.shape[2]
    mod = lambda: pl.BlockSpec((None, 1, d), lambda i, sl: (i // n_blk, 0, 0))
    grid_spec = pltpu.PrefetchScalarGridSpec(
        num_scalar_prefetch=1,
        grid=(t // tm,),
        in_specs=[
            pl.BlockSpec((tm, d), lambda i, sl: (i, 0)),
            mod(), mod(),
            pl.BlockSpec((None, d, d), lambda i, sl: (layer, 0, 0)),
            pl.BlockSpec((None, 3, rows, d), lambda i, sl: (i // n_blk, 0, 0, 0)),
        ],
        out_specs=pl.BlockSpec((None, N_HEADS, tm, AUG), lambda i, sl: (i // n_blk, 0, i % n_blk, 0)),
    )
    return pl.pallas_call(
        functools.partial(_qaug_kernel, n_blk=n_blk),
        grid_spec=grid_spec,
        out_shape=jax.ShapeDtypeStruct((batch, N_HEADS, seq, AUG), BF16),
        compiler_params=_params(("arbitrary",)),
        name="q_aug",
    )(slopes, x, sc, sh, w, kmbd)


def _block_diag_means(km, n_blk):
    d = km.shape[-1]
    r = jnp.arange(N_HEADS * n_blk)
    head_of_col = jnp.arange(d) // HEAD_DIM
    bd = jnp.where((r // n_blk)[None, :, None] == head_of_col[None, None, :], km[:, r % n_blk, :], 0.0)
    return jnp.stack(_split3(bd), axis=1).astype(BF16)


def _attn_kernel(qlo_ref, qhi_ref, ka_ref, va_ref, olo_ref, ohi_ref, q_scr, s_scr, *, n_blk):
    i = pl.program_id(2)
    bq = MOBA_BLOCK
    half = n_blk // 2
    q_scr[0] = qlo_ref[...]
    q_scr[1] = qhi_ref[...]
    dmat = lax.broadcasted_iota(jnp.int32, (bq, bq), 1) - lax.broadcasted_iota(jnp.int32, (bq, bq), 0)

    def unit(t):
        if t >= half:
            return None, 1, pl.multiple_of((t - i - 1) * bq, bq)
        is_lo = i >= t
        key = jnp.where(is_lo, t, t - i - 1)
        return is_lo, jnp.where(is_lo, 0, 1), pl.multiple_of(key * bq, bq)

    neg = jnp.full((bq, LANES), NEG_INF, F32)
    mx = [[neg] * HEAD_PAIR, [neg] * HEAD_PAIR]
    for t in range(n_blk + 1):
        is_lo, sel, start = unit(t)
        if is_lo is not None:
            causal = dmat <= jnp.where(is_lo, i - t, n_blk - t) * bq
        elif t == n_blk:
            causal = dmat <= 0
        else:
            causal = None
        for hh in range(HEAD_PAIR):
            s = _dot_nt(q_scr[sel, hh], ka_ref[hh, pl.ds(start, bq), :])
            if causal is not None:
                s = jnp.where(causal, s, NEG_INF)
            s_scr[hh, t] = s
            smax = jnp.maximum(s[:, :LANES], s[:, LANES:])
            if is_lo is None:
                mx[1][hh] = jnp.maximum(mx[1][hh], smax)
            else:
                mx[0][hh] = jnp.maximum(mx[0][hh], jnp.where(is_lo, smax, NEG_INF))
                mx[1][hh] = jnp.maximum(mx[1][hh], jnp.where(is_lo, NEG_INF, smax))

    m_rows = [[jnp.broadcast_to(jnp.max(mx[sel][hh], axis=1, keepdims=True), (bq, LANES))
               for hh in range(HEAD_PAIR)] for sel in range(2)]
    zero = jnp.zeros((bq, AUG), F32)
    acc = [[zero] * HEAD_PAIR, [zero] * HEAD_PAIR]
    for t in range(n_blk + 1):
        is_lo, _, start = unit(t)
        for hh in range(HEAD_PAIR):
            m_row = m_rows[1][hh] if is_lo is None else jnp.where(is_lo, m_rows[0][hh], m_rows[1][hh])
            p = jnp.exp2(s_scr[hh, t] - jnp.concatenate([m_row, m_row], axis=1)).astype(BF16)
            pv = jnp.dot(p, va_ref[hh, pl.ds(start, bq), :], preferred_element_type=F32)
            if is_lo is None:
                acc[1][hh] = acc[1][hh] + pv
            else:
                acc[0][hh] = acc[0][hh] + jnp.where(is_lo, pv, 0.0)
                acc[1][hh] = acc[1][hh] + jnp.where(is_lo, 0.0, pv)

    for sel, o_ref in enumerate((olo_ref, ohi_ref)):
        outs = [a[:, :HEAD_DIM] / a[:, HEAD_DIM:HEAD_DIM + 1] for a in acc[sel]]
        o_ref[...] = jnp.concatenate(outs, axis=1).astype(o_ref.dtype)


def _attention(qa, ka, va, batch, seq):
    n_blk = seq // MOBA_BLOCK
    assert n_blk % 2 == 0
    pair = lambda: pl.BlockSpec((None, HEAD_PAIR, seq, AUG), lambda b, hp, i: (b, hp, 0, 0))
    half = n_blk // 2
    o_shape = jax.ShapeDtypeStruct((batch, seq // 2, D_MODEL), BF16)
    out_lo, out_hi = pl.pallas_call(
        functools.partial(_attn_kernel, n_blk=n_blk),
        grid=(batch, N_HEADS // HEAD_PAIR, n_blk // 2),
        in_specs=[
            pl.BlockSpec((None, HEAD_PAIR, MOBA_BLOCK, AUG), lambda b, hp, i: (b, hp, i, 0)),
            pl.BlockSpec((None, HEAD_PAIR, MOBA_BLOCK, AUG), lambda b, hp, i: (b, hp, n_blk - 1 - i, 0)),
            pair(), pair(),
        ],
        out_specs=[
            pl.BlockSpec((None, MOBA_BLOCK, HEAD_PAIR * HEAD_DIM), lambda b, hp, i: (b, i, hp)),
            pl.BlockSpec((None, MOBA_BLOCK, HEAD_PAIR * HEAD_DIM), lambda b, hp, i: (b, half - 1 - i, hp)),
        ],
        out_shape=[o_shape, o_shape],
        scratch_shapes=[
            pltpu.VMEM((2, HEAD_PAIR, MOBA_BLOCK, AUG), BF16),
            pltpu.VMEM((HEAD_PAIR, n_blk + 1, MOBA_BLOCK, MOBA_BLOCK), F32),
        ],
        compiler_params=_params(("arbitrary", "arbitrary", "arbitrary")),
        name="moba_attn",
    )(qa, qa, ka, va)
    return out_lo, out_hi


def kernel(x, c, ada_w, ada_b, ln_g, ln_b, conv_in_w, conv_in_b, conv_dw_w, conv_dw_b, conv_norm_g,
           conv_norm_b, conv_out_w, conv_out_b, kv_ada_w, kv_ada_b, w_kv, w_q, w_o, ffn_w13, ffn_w2,
           router_w, router_b, moe_w13, moe_w2):
    batch, seq, d = x.shape
    t = batch * seq
    assert d == D_MODEL and seq % ROW_TILE == 0 and seq % MOBA_BLOCK == 0 and t & (t - 1) == 0
    n_blk = seq // MOBA_BLOCK
    assert n_blk <= 16

    c_pad = jnp.pad(c, ((0, SUBLANES - batch), (0, 0)))
    mods = _ada(c_pad, ada_w, ada_b, 1536)[:, :batch]
    kv_mod = _ada(c_pad, kv_ada_w[None], kv_ada_b[None], 1024)[0, :batch]

    def mod_vec(v):
        return v.reshape(batch, 1, d)

    row = lambda v: v.reshape(1, -1)
    slopes = jnp.exp2(-8.0 * jnp.arange(1, N_HEADS + 1, dtype=F32) / N_HEADS)
    n_moe_tiles = 2 * t // MOE_TILE + N_EXPERTS
    conv_in_wb, conv_out_wb, w_qb, w_ob = [w.astype(BF16) for w in (conv_in_w, conv_out_w, w_q, w_o)]
    ffn_w13b, ffn_w2b, moe_w13b, moe_w2b = [w.astype(BF16) for w in (ffn_w13, ffn_w2, moe_w13, moe_w2)]

    xf = x.reshape(t, d)
    ka = va = kmbd = None
    for l in range(DEPTH):
        if l == N_A_LAYERS:
            ka, va, km = _shared_kv(xf, mod_vec(kv_mod[:, d:]), mod_vec(kv_mod[:, :d]),
                                    w_kv.astype(BF16), batch, seq)
            kmbd = _block_diag_means(km.reshape(batch, n_blk, d), n_blk)
        sh1, sc1, g1, sh2, sc2, g2 = [mod_vec(mods[l, :, k * d:(k + 1) * d]) for k in range(6)]
        lng1, lnb1 = row(ln_g[l, 0]), row(ln_b[l, 0])
        lng2, lnb2 = row(ln_g[l, 1]), row(ln_b[l, 1])
        if l < N_A_LAYERS:
            u = _conv_in(xf, sc1, sh1, conv_in_wb, l, row(conv_in_b[l]), seq)
            xf = _conv_block(u, xf, g1, conv_dw_w[l], row(conv_dw_b[l]), row(conv_norm_g[l]),
                             row(conv_norm_b[l]), conv_out_wb, l, row(conv_out_b[l]), lng1, lnb1, seq)
        else:
            jl = l - N_A_LAYERS
            qa = _q_aug(slopes, xf, sc1, sh1, w_qb, jl, kmbd, batch, seq)
            att_lo, att_hi = _attention(qa, ka, va, batch, seq)
            xf = _oproj(att_lo, att_hi, xf, g1, w_ob, jl, lng1, lnb1, seq)
        if l % 2 == 0:
            xf = _ffn(xf, sc2, sh2, g2, ffn_w13b, ffn_w2b, l // 2, lng2, lnb2, seq)
        else:
            e = l // 2
            h, meta_rows, meta_cols = _router(xf, sc2, sh2, router_w[e].T, router_b[e].reshape(-1, 1), seq)
            te, nvalid, row_dst = _route_rows(meta_rows[0].astype(jnp.int32), meta_rows[1].astype(jnp.int32),
                                              MOE_TILE, n_moe_tiles)
            y = _moe(h, te, nvalid, row_dst, moe_w13b, moe_w2b, e)
            xf = _combine(xf, y, meta_cols, g2, lng2, lnb2, seq)
    return xf.reshape(batch, seq, d)
```

```python
import functools

import jax
import jax.numpy as jnp
from jax import lax
from jax.experimental import pallas as pl
from jax.experimental.pallas import tpu as pltpu

F32 = jnp.float32
BF16 = jnp.bfloat16
HIGHEST = lax.Precision.HIGHEST

D_MODEL = 1024
DEPTH = 4
N_A_LAYERS = DEPTH // 2
CONV_K = 31
N_HEADS = 16
HEAD_DIM = D_MODEL // N_HEADS
MOBA_BLOCK = 256
MOBA_TOPK = 3
D_FF = 2816
N_EXPERTS = 8
D_FF_EXPERT = 3584
ALPHA = (2.0 * DEPTH) ** 0.25
LN_EPS = 1e-5
NEG_INF = -1e30

SUBLANES = 8
LANES = 128
VMEM_LIMIT = 48 * 1024 * 1024

ROW_TILE = 512
CONV_HALO = 32
CONV_TILE = 256
CONV_CHUNK = 32
FFN_CHUNK = 1408
MOE_CHUNK = 1792
MOE_TILE = 512
HEAD_PAIR = 2
AUG = 2 * HEAD_DIM
EXT = 32
LOG2E = 1.4426950408889634


def _params(sem, vmem=VMEM_LIMIT):
    return pltpu.CompilerParams(dimension_semantics=sem, vmem_limit_bytes=vmem)


def _layer_norm(y, g, b):
    mu = jnp.mean(y, axis=-1, keepdims=True)
    yc = y - mu
    var = jnp.mean(yc * yc, axis=-1, keepdims=True)
    return yc * lax.rsqrt(var + LN_EPS) * g + b


def _modulate(x, sc, sh):
    return x * (1.0 + sc) + sh


def _ada_kernel(c_ref, w_ref, b_ref, o_ref):
    ca = jax.nn.silu(c_ref[...])
    o_ref[...] = jnp.dot(ca, w_ref[...], precision=HIGHEST, preferred_element_type=F32) + b_ref[...]


def _ada(c_pad, w, b, tn):
    n_l, d, n = w.shape
    rows = c_pad.shape[0]
    return pl.pallas_call(
        _ada_kernel,
        grid=(n_l, n // tn),
        in_specs=[
            pl.BlockSpec((rows, d), lambda l, j: (0, 0)),
            pl.BlockSpec((None, d, tn), lambda l, j: (l, 0, j)),
            pl.BlockSpec((None, 1, tn), lambda l, j: (l, 0, j)),
        ],
        out_specs=pl.BlockSpec((None, rows, tn), lambda l, j: (l, 0, j)),
        out_shape=jax.ShapeDtypeStruct((n_l, rows, n), F32),
        compiler_params=_params(("arbitrary", "arbitrary")),
        name="ada",
    )(c_pad, w, b.reshape(n_l, 1, n))


def _conv_in_kernel(x_ref, sc_ref, sh_ref, w_ref, b_ref, o_ref):
    h = _modulate(x_ref[...], sc_ref[...], sh_ref[...]).astype(BF16)
    y = jnp.dot(h, w_ref[...], preferred_element_type=F32) + b_ref[...]
    half = y.shape[1] // 2
    o_ref[...] = y[:, :half] * jax.nn.sigmoid(y[:, half:])


def _conv_in(x, sc, sh, w, layer, b, seq):
    t, d = x.shape
    n = w.shape[2]
    n_out = n // 2
    tm = ROW_TILE
    per_seq = seq // tm
    return pl.pallas_call(
        _conv_in_kernel,
        grid=(t // tm,),
        in_specs=[
            pl.BlockSpec((tm, d), lambda i: (i, 0)),
            pl.BlockSpec((None, 1, d), lambda i: (i // per_seq, 0, 0)),
            pl.BlockSpec((None, 1, d), lambda i: (i // per_seq, 0, 0)),
            pl.BlockSpec((None, d, n), lambda i: (layer, 0, 0)),
            pl.BlockSpec((1, n), lambda i: (0, 0)),
        ],
        out_specs=pl.BlockSpec((tm, n_out), lambda i: (i, 0)),
        out_shape=jax.ShapeDtypeStruct((t, n_out), F32),
        compiler_params=_params(("arbitrary",)),
        name="conv_in",
    )(x, sc, sh, w, b)


def _conv_kernel(uprev_ref, u_ref, x_ref, g1_ref, wdw_ref, bdw_ref, gn_ref, bn_ref, wo_ref, bo_ref,
                 lng_ref, lnb_ref, o_ref, win_ref, cv_ref, *, per_seq):
    tm = u_ref.shape[0]
    rows = CONV_HALO + tm
    first = (pl.program_id(0) % per_seq) == 0
    win_ref[0, 0:CONV_HALO, :] = jnp.where(first, 0.0, uprev_ref[...])
    win_ref[0, CONV_HALO:, :] = u_ref[...]
    groups = rows // SUBLANES - 1
    cur = win_ref[0, SUBLANES:, :].reshape(groups, SUBLANES, D_MODEL)
    prev = win_ref[0, 0:rows - SUBLANES, :].reshape(groups, SUBLANES, D_MODEL)
    sub = lax.broadcasted_iota(jnp.int32, (groups, SUBLANES, D_MODEL), 1)
    for b in range(1, SUBLANES):
        mixed = jnp.where(sub < SUBLANES - b, cur, prev)
        win_ref[b, SUBLANES:, :] = pltpu.roll(mixed, b, axis=1).reshape(rows - SUBLANES, D_MODEL)
    for c in range(tm // CONV_CHUNK):
        base = CONV_HALO + c * CONV_CHUNK
        acc = jnp.zeros((CONV_CHUNK, D_MODEL), F32) + bdw_ref[...]
        for s in range(CONV_K):
            a, b = divmod(s, SUBLANES)
            k = CONV_K - 1 - s
            acc = acc + win_ref[b, base - SUBLANES * a:base - SUBLANES * a + CONV_CHUNK, :] * wdw_ref[k:k + 1, :]
        cv_ref[c * CONV_CHUNK:(c + 1) * CONV_CHUNK, :] = acc
    un = jax.nn.silu(_layer_norm(cv_ref[...], gn_ref[...], bn_ref[...])).astype(BF16)
    out = jnp.dot(un, wo_ref[...], preferred_element_type=F32) + bo_ref[...]
    y = ALPHA * x_ref[...] + (1.0 + g1_ref[...]) * out
    o_ref[...] = _layer_norm(y, lng_ref[...], lnb_ref[...])


def _conv_block(u, x, g1, wdw, bdw, gn, bn, wo, layer, bo, lng, lnb, seq):
    t, d = x.shape
    tm = CONV_TILE
    per_seq = seq // tm
    halo_per_tile = tm // CONV_HALO
    vec = lambda: pl.BlockSpec((1, d), lambda i: (0, 0))
    return pl.pallas_call(
        functools.partial(_conv_kernel, per_seq=per_seq),
        grid=(t // tm,),
        in_specs=[
            pl.BlockSpec((CONV_HALO, d), lambda i: (jnp.maximum(i * halo_per_tile - 1, 0), 0)),
            pl.BlockSpec((tm, d), lambda i: (i, 0)),
            pl.BlockSpec((tm, d), lambda i: (i, 0)),
            pl.BlockSpec((None, 1, d), lambda i: (i // per_seq, 0, 0)),
            pl.BlockSpec((CONV_K, d), lambda i: (0, 0)),
            vec(), vec(), vec(),
            pl.BlockSpec((None, d, d), lambda i: (layer, 0, 0)),
            vec(), vec(), vec(),
        ],
        out_specs=pl.BlockSpec((tm, d), lambda i: (i, 0)),
        out_shape=jax.ShapeDtypeStruct((t, d), F32),
        scratch_shapes=[pltpu.VMEM((SUBLANES, CONV_HALO + tm, d), F32), pltpu.VMEM((tm, d), F32)],
        compiler_params=_params(("arbitrary",)),
        name="conv_block",
    )(u, u, x, g1, wdw, bdw, gn, bn, wo, bo, lng, lnb)


def _oproj_kernel(alo_ref, ahi_ref, x_ref, g1_ref, w_ref, lng_ref, lnb_ref, o_ref, *, per_seq):
    first_half = (pl.program_id(0) % per_seq) < per_seq // 2
    a = jnp.where(first_half, alo_ref[...], ahi_ref[...])
    out = jnp.dot(a, w_ref[...], preferred_element_type=F32)
    y = ALPHA * x_ref[...] + (1.0 + g1_ref[...]) * out
    o_ref[...] = _layer_norm(y, lng_ref[...], lnb_ref[...])


def _oproj(a_lo, a_hi, x, g1, w, layer, lng, lnb, seq):
    t, d = x.shape
    tm = ROW_TILE
    per_seq = seq // tm
    half = per_seq // 2
    vec = lambda: pl.BlockSpec((1, d), lambda i: (0, 0))
    return pl.pallas_call(
        functools.partial(_oproj_kernel, per_seq=per_seq),
        grid=(t // tm,),
        in_specs=[
            pl.BlockSpec((None, tm, d), lambda i: (i // per_seq, jnp.minimum(i % per_seq, half - 1), 0)),
            pl.BlockSpec((None, tm, d), lambda i: (i // per_seq, jnp.maximum(i % per_seq - half, 0), 0)),
            pl.BlockSpec((tm, d), lambda i: (i, 0)),
            pl.BlockSpec((None, 1, d), lambda i: (i // per_seq, 0, 0)),
            pl.BlockSpec((None, d, d), lambda i: (layer, 0, 0)),
            vec(), vec(),
        ],
        out_specs=pl.BlockSpec((tm, d), lambda i: (i, 0)),
        out_shape=jax.ShapeDtypeStruct((t, d), F32),
        compiler_params=_params(("arbitrary",)),
        name="oproj",
    )(a_lo, a_hi, x, g1, w, lng, lnb)


def _swiglu_chunk(hb, wa_ref, wb_ref, w2_ref):
    a = jnp.dot(hb, wa_ref[...], preferred_element_type=F32)
    b = jnp.dot(hb, wb_ref[...], preferred_element_type=F32)
    g = (jax.nn.silu(a) * b).astype(BF16)
    return jnp.dot(g, w2_ref[...], preferred_element_type=F32)


def _ffn_kernel(x_ref, sc_ref, sh_ref, g2_ref, wa_ref, wb_ref, w2_ref, lng_ref, lnb_ref, o_ref,
                hb_ref, acc_ref):
    j = pl.program_id(1)

    @pl.when(j == 0)
    def _():
        hb_ref[...] = _modulate(x_ref[...], sc_ref[...], sh_ref[...]).astype(BF16)
        acc_ref[...] = jnp.zeros_like(acc_ref)

    acc_ref[...] += _swiglu_chunk(hb_ref[...], wa_ref, wb_ref, w2_ref)

    @pl.when(j == pl.num_programs(1) - 1)
    def _():
        y = ALPHA * x_ref[...] + (1.0 + g2_ref[...]) * acc_ref[...]
        o_ref[...] = _layer_norm(y, lng_ref[...], lnb_ref[...])


def _ffn(x, sc, sh, g2, w13, w2, layer, lng, lnb, seq):
    t, d = x.shape
    f = w2.shape[1]
    tm, fc = ROW_TILE, FFN_CHUNK
    n_fc = f // fc
    per_seq = seq // tm
    mod = lambda: pl.BlockSpec((None, 1, d), lambda i, j: (i // per_seq, 0, 0))
    vec = lambda: pl.BlockSpec((1, d), lambda i, j: (0, 0))
    return pl.pallas_call(
        _ffn_kernel,
        grid=(t // tm, n_fc),
        in_specs=[
            pl.BlockSpec((tm, d), lambda i, j: (i, 0)),
            mod(), mod(), mod(),
            pl.BlockSpec((None, d, fc), lambda i, j: (layer, 0, j)),
            pl.BlockSpec((None, d, fc), lambda i, j: (layer, 0, j + n_fc)),
            pl.BlockSpec((None, fc, d), lambda i, j: (layer, j, 0)),
            vec(), vec(),
        ],
        out_specs=pl.BlockSpec((tm, d), lambda i, j: (i, 0)),
        out_shape=jax.ShapeDtypeStruct((t, d), F32),
        scratch_shapes=[pltpu.VMEM((tm, d), BF16), pltpu.VMEM((tm, d), F32)],
        compiler_params=_params(("arbitrary", "arbitrary")),
        name="ffn",
    )(x, sc, sh, g2, w13, w13, w2, lng, lnb)


def _router_kernel(x_ref, sc_ref, sh_ref, rwt_ref, rb_ref, h_ref, rows_ref, cols_ref):
    h = _modulate(x_ref[...], sc_ref[...], sh_ref[...])
    h_ref[...] = h.reshape(h.shape[0], SUBLANES, LANES)
    lg = lax.dot_general(rwt_ref[...], h, (((1,), (1,)), ((), ())), precision=HIGHEST,
                         preferred_element_type=F32) + rb_ref[...]
    idx = lax.broadcasted_iota(jnp.int32, lg.shape, 0)
    m1 = jnp.max(lg, axis=0, keepdims=True)
    i1 = jnp.min(jnp.where(lg == m1, idx, N_EXPERTS), axis=0, keepdims=True)
    lg2 = jnp.where(idx == i1, -jnp.inf, lg)
    m2 = jnp.max(lg2, axis=0, keepdims=True)
    i2 = jnp.min(jnp.where(lg2 == m2, idx, N_EXPERTS), axis=0, keepdims=True)
    e2 = jnp.exp(m2 - m1)
    den = 1.0 + e2
    w1 = 1.0 / den
    w2 = e2 / den
    meta = jnp.where(idx == 0, i1.astype(F32),
                     jnp.where(idx == 1, i2.astype(F32),
                               jnp.where(idx == 2, w1, jnp.where(idx == 3, w2, 0.0))))
    rows_ref[...] = meta
    wide = jnp.concatenate([meta, jnp.zeros((LANES - N_EXPERTS, meta.shape[1]), F32)], axis=0)
    cols_ref[...] = wide.T


def _router(x, sc, sh, rwt, rb, seq):
    t, d = x.shape
    tm = ROW_TILE
    per_seq = seq // tm
    mod = lambda: pl.BlockSpec((None, 1, d), lambda i: (i // per_seq, 0, 0))
    return pl.pallas_call(
        _router_kernel,
        grid=(t // tm,),
        in_specs=[
            pl.BlockSpec((tm, d), lambda i: (i, 0)),
            mod(), mod(),
            pl.BlockSpec((N_EXPERTS, d), lambda i: (0, 0)),
            pl.BlockSpec((N_EXPERTS, 1), lambda i: (0, 0)),
        ],
        out_specs=[
            pl.BlockSpec((tm, SUBLANES, LANES), lambda i: (i, 0, 0)),
            pl.BlockSpec((N_EXPERTS, tm), lambda i: (0, i)),
            pl.BlockSpec((tm, LANES), lambda i: (i, 0)),
        ],
        out_shape=[
            jax.ShapeDtypeStruct((t, SUBLANES, LANES), F32),
            jax.ShapeDtypeStruct((N_EXPERTS, t), F32),
            jax.ShapeDtypeStruct((t, LANES), F32),
        ],
        compiler_params=_params(("arbitrary",)),
        name="router",
    )(x, sc, sh, rwt, rb)


def _moe_kernel(te_ref, nv_ref, tcur_ref, tnext_ref, dprev_ref, h_hbm, wa_ref, wb_ref, w2_ref, y_hbm,
                hbuf, hb_ref, obuf, gsem, ssem, *, n_fc):
    i = pl.program_id(0)
    j = pl.program_id(1)
    n_tiles = pl.num_programs(0)
    rows_per_step = hbuf.shape[1]
    d = D_MODEL
    tm = n_fc * rows_per_step
    slot = i % 2
    prev_slot = 1 - slot

    def gather_copy(tok, step, r):
        return pltpu.make_async_copy(h_hbm.at[pl.ds(tok, 1)], hbuf.at[step, pl.ds(r, 1)], gsem.at[0])

    def scatter_copy(step, r, dst):
        return pltpu.make_async_copy(obuf.at[prev_slot, step, pl.ds(r, 1)], y_hbm.at[pl.ds(dst, 1)], ssem.at[0])

    def wait_gather():
        pltpu.make_async_copy(hbuf, hbuf, gsem.at[0]).wait()

    def wait_scatter():
        pltpu.make_async_copy(obuf.at[0], obuf.at[0], ssem.at[0]).wait()

    @pl.when(j == 0)
    def _():
        @pl.when(i == 0)
        def _():
            obuf[...] = jnp.zeros_like(obuf)

            def body(row, carry):
                gather_copy(tcur_ref[0, row], row // rows_per_step, row % rows_per_step).start()
                return carry
            lax.fori_loop(0, tm, body, 0)

        wait_gather()
        hb_ref[...] = hbuf[...].reshape(tm, d).astype(BF16)

        @pl.when(i >= 1)
        def _():
            wait_scatter()

    def issue_row_dmas():
        base = j * rows_per_step
        for r in range(rows_per_step):
            gather_copy(tnext_ref[0, base + r], j, r).start()
            scatter_copy(j, r, dprev_ref[0, base + r]).start()

    used = nv_ref[i] > 0

    @pl.when(used)
    def _():
        contrib = _swiglu_chunk(hb_ref[...], wa_ref, wb_ref, w2_ref).reshape(n_fc, rows_per_step, SUBLANES, LANES)
        issue_row_dmas()

        @pl.when(j == 0)
        def _():
            obuf[slot] = contrib

        @pl.when(j > 0)
        def _():
            obuf[slot] += contrib

    @pl.when(jnp.logical_not(used))
    def _():
        issue_row_dmas()

    @pl.when((j == n_fc - 1) & (i == n_tiles - 1))
    def _():
        wait_gather()
        wait_scatter()


def _moe(h, tile_expert, tile_nvalid, row_dst, w13, w2, layer):
    t, d = h.shape[0], D_MODEL
    n_tiles, _, tm = row_dst.shape
    f = w2.shape[2]
    fc = MOE_CHUNK
    n_fc = f // fc
    lead = (2 * t + tm + jnp.arange(tm, dtype=jnp.int32)).reshape(1, 1, tm)
    prev_dst = jnp.concatenate([lead, row_dst[:-1]], axis=0)
    row_tok = row_dst & (t - 1)

    def chunk(i, j, nv):
        return jnp.where(nv[i] > 0, j, n_fc - 1)

    grid_spec = pltpu.PrefetchScalarGridSpec(
        num_scalar_prefetch=2,
        grid=(n_tiles, n_fc),
        in_specs=[
            pl.BlockSpec((None, 1, tm), lambda i, j, te, nv: (i, 0, 0), memory_space=pltpu.SMEM),
            pl.BlockSpec((None, 1, tm), lambda i, j, te, nv: (jnp.minimum(i + 1, n_tiles - 1), 0, 0),
                         memory_space=pltpu.SMEM),
            pl.BlockSpec((None, 1, tm), lambda i, j, te, nv: (i, 0, 0), memory_space=pltpu.SMEM),
            pl.BlockSpec(memory_space=pl.ANY),
            pl.BlockSpec((None, None, d, fc), lambda i, j, te, nv: (layer, te[i], 0, chunk(i, j, nv))),
            pl.BlockSpec((None, None, d, fc), lambda i, j, te, nv: (layer, te[i], 0, chunk(i, j, nv) + n_fc)),
            pl.BlockSpec((None, None, fc, d), lambda i, j, te, nv: (layer, te[i], chunk(i, j, nv), 0)),
        ],
        out_specs=pl.BlockSpec(memory_space=pl.ANY),
        scratch_shapes=[
            pltpu.VMEM((n_fc, tm // n_fc, SUBLANES, LANES), F32),
            pltpu.VMEM((tm, d), BF16),
            pltpu.VMEM((2, n_fc, tm // n_fc, SUBLANES, LANES), F32),
            pltpu.SemaphoreType.DMA((1,)),
            pltpu.SemaphoreType.DMA((1,)),
        ],
    )
    return pl.pallas_call(
        functools.partial(_moe_kernel, n_fc=n_fc),
        grid_spec=grid_spec,
        out_shape=jax.ShapeDtypeStruct((2 * t + 2 * tm, SUBLANES, LANES), F32),
        compiler_params=_params(("arbitrary", "arbitrary")),
        name="moe",
    )(tile_expert, tile_nvalid, row_tok, row_tok, prev_dst, h, w13, w13, w2)


def _route_rows(e1, e2, tm, n_tiles):
    t = e1.shape[0]
    experts = jnp.concatenate([e1, e2])
    order = jnp.argsort(experts, stable=True).astype(jnp.int32)
    counts = jnp.sum(experts[:, None] == jnp.arange(N_EXPERTS)[None, :], axis=0).astype(jnp.int32)
    tiles_per = (counts + tm - 1) // tm
    tile_end = jnp.cumsum(tiles_per)
    tile_start = tile_end - tiles_per
    group_start = jnp.cumsum(counts) - counts
    tile_ids = jnp.arange(n_tiles, dtype=jnp.int32)
    te = jnp.minimum(jnp.sum(tile_ids[:, None] >= tile_end[None, :], axis=1), N_EXPERTS - 1).astype(jnp.int32)
    used = tile_ids < tile_end[-1]
    first_row = (tile_ids - tile_start[te]) * tm
    nvalid = jnp.where(used, jnp.clip(counts[te] - first_row, 0, tm), 0).astype(jnp.int32)
    last_e = te[jnp.maximum(tile_end[-1] - 1, 0)]
    te = jnp.where(used, te, last_e)
    r = jnp.arange(tm, dtype=jnp.int32)
    src = group_start[te][:, None] + first_row[:, None] + r[None, :]
    valid = r[None, :] < nvalid[:, None]
    dump = 2 * t + (tile_ids % 2)[:, None] * tm + r[None, :]
    row_dst = jnp.where(valid, order[jnp.clip(src, 0, 2 * t - 1)], dump).astype(jnp.int32)
    return te, nvalid, row_dst.reshape(n_tiles, 1, tm)


def _combine_kernel(x_ref, y0_ref, y1_ref, gw_ref, g2_ref, lng_ref, lnb_ref, o_ref):
    gw = gw_ref[...]
    rows = x_ref.shape
    f = gw[:, 2:3] * y0_ref[...].reshape(rows) + gw[:, 3:4] * y1_ref[...].reshape(rows)
    y = ALPHA * x_ref[...] + (1.0 + g2_ref[...]) * f
    o_ref[...] = _layer_norm(y, lng_ref[...], lnb_ref[...])


def _combine(x, y, gw, g2, lng, lnb, seq):
    t, d = x.shape
    tm = ROW_TILE
    per_seq = seq // tm
    n_t = t // tm
    vec = lambda: pl.BlockSpec((1, d), lambda i: (0, 0))
    return pl.pallas_call(
        _combine_kernel,
        grid=(n_t,),
        in_specs=[
            pl.BlockSpec((tm, d), lambda i: (i, 0)),
            pl.BlockSpec((tm, SUBLANES, LANES), lambda i: (i, 0, 0)),
            pl.BlockSpec((tm, SUBLANES, LANES), lambda i: (i + n_t, 0, 0)),
            pl.BlockSpec((tm, LANES), lambda i: (i, 0)),
            pl.BlockSpec((None, 1, d), lambda i: (i // per_seq, 0, 0)),
            vec(), vec(),
        ],
        out_specs=pl.BlockSpec((tm, d), lambda i: (i, 0)),
        out_shape=jax.ShapeDtypeStruct((t, d), F32),
        compiler_params=_params(("arbitrary",)),
        name="combine",
    )(x, y, y, gw, g2, lng, lnb)


def _kv_kernel(x_ref, sc_ref, sh_ref, w_ref, ka_ref, va_ref, km_ref, *, blocks_per_seq):
    h = _modulate(x_ref[...], sc_ref[...], sh_ref[...]).astype(BF16)
    kv = jnp.dot(h, w_ref[...], preferred_element_type=F32)
    k = kv[:, :D_MODEL]
    v = kv[:, D_MODEL:]
    km_ref[...] = jnp.mean(k, axis=0, keepdims=True)
    j = pl.program_id(0) % blocks_per_seq
    shape = (MOBA_BLOCK, HEAD_DIM)
    col = lax.broadcasted_iota(jnp.int32, shape, 1)
    pos = lax.broadcasted_iota(jnp.int32, shape, 0).astype(F32)
    blk = (j * MOBA_BLOCK).astype(F32)
    kx = jnp.where(col < 3, pos,
                   jnp.where(col < 6, blk,
                             jnp.where((col >= 8) & (col < 11), 1.0,
                                       jnp.where(col == 16 + j, 1.0, 0.0)))).astype(BF16)
    vx = jnp.where(col == 0, 1.0, 0.0).astype(BF16)
    for hd in range(N_HEADS):
        lo = hd * HEAD_DIM
        ka_ref[hd] = jnp.concatenate([k[:, lo:lo + HEAD_DIM].astype(BF16), kx], axis=1)
        va_ref[hd] = jnp.concatenate([v[:, lo:lo + HEAD_DIM].astype(BF16), vx], axis=1)


def _shared_kv(x, sc, sh, w, batch, seq):
    t, d = x.shape
    tm = MOBA_BLOCK
    n_blk = seq // tm
    mod = lambda: pl.BlockSpec((None, 1, d), lambda i: (i // n_blk, 0, 0))
    aug = lambda: pl.BlockSpec((None, N_HEADS, tm, AUG), lambda i: (i // n_blk, 0, i % n_blk, 0))
    return pl.pallas_call(
        functools.partial(_kv_kernel, blocks_per_seq=n_blk),
        grid=(t // tm,),
        in_specs=[
            pl.BlockSpec((tm, d), lambda i: (i, 0)),
            mod(), mod(),
            pl.BlockSpec((d, 2 * d), lambda i: (0, 0)),
        ],
        out_specs=[aug(), aug(), pl.BlockSpec((None, None, 1, d), lambda i: (i // n_blk, i % n_blk, 0, 0))],
        out_shape=[
            jax.ShapeDtypeStruct((batch, N_HEADS, seq, AUG), BF16),
            jax.ShapeDtypeStruct((batch, N_HEADS, seq, AUG), BF16),
            jax.ShapeDtypeStruct((batch, n_blk, 1, d), F32),
        ],
        compiler_params=_params(("arbitrary",)),
        name="shared_kv",
    )(x, sc, sh, w)


def _split3(x):
    p1 = x.astype(BF16).astype(F32)
    p2 = (x - p1).astype(BF16).astype(F32)
    p3 = (x - p1 - p2).astype(BF16).astype(F32)
    return p1, p2, p3


def _dot_nt(a, b):
    return lax.dot_general(a, b, (((1,), (1,)), ((), ())), preferred_element_type=F32)


def _qaug_kernel(sl_ref, x_ref, sc_ref, sh_ref, w_ref, kmbd_ref, qa_ref, *, n_blk):
    i = pl.program_id(0) % n_blk
    bq = MOBA_BLOCK
    h = _modulate(x_ref[...], sc_ref[...], sh_ref[...]).astype(BF16)
    qb = (jnp.dot(h, w_ref[...], preferred_element_type=F32) * (HEAD_DIM ** -0.5 * LOG2E)).astype(BF16)
    gates = _dot_nt(kmbd_ref[0], qb) + _dot_nt(kmbd_ref[1], qb) + _dot_nt(kmbd_ref[2], qb)
    lane_q = lax.broadcasted_iota(jnp.int32, (SUBLANES, bq), 1).astype(F32)
    sub8 = lax.broadcasted_iota(jnp.int32, (SUBLANES, bq), 0)
    blk_id = lax.broadcasted_iota(jnp.int32, (n_blk, bq), 0)
    t_q = lane_q + (i * bq).astype(F32)
    group = LANES // EXT
    for g0 in range(0, N_HEADS, group):
        exts = []
        for hd in range(g0, g0 + group):
            gate = jnp.where(blk_id < i, gates[hd * n_blk:(hd + 1) * n_blk, :], -jnp.inf)
            rank = jnp.zeros((n_blk, bq), jnp.int32)
            for m in range(n_blk):
                gm = gate[m:m + 1, :]
                ahead = (gm > gate) | ((gm == gate) & (blk_id > m))
                rank = rank + ahead.astype(jnp.int32)
            bias = jnp.where((blk_id < i) & (rank >= MOBA_TOPK), NEG_INF, 0.0)
            slope = jnp.full((SUBLANES, bq), sl_ref[hd], F32) * LOG2E
            s1, s2, s3 = _split3(slope)
            t1, t2, t3 = _split3(-slope * t_q)
            rows_s = jnp.where((sub8 == 0) | (sub8 == 3), s1,
                               jnp.where((sub8 == 1) | (sub8 == 4), s2,
                                         jnp.where((sub8 == 2) | (sub8 == 5), s3, 0.0)))
            rows_t = jnp.where(sub8 == 0, t1, jnp.where(sub8 == 1, t2, jnp.where(sub8 == 2, t3, 0.0)))
            pieces = [rows_s, rows_t, bias]
            if EXT - 2 * SUBLANES - n_blk:
                pieces.append(jnp.zeros((EXT - 2 * SUBLANES - n_blk, bq), F32))
            exts.extend(pieces)
        ext_t = jnp.concatenate(exts, axis=0).T.astype(BF16)
        pad = jnp.zeros((bq, AUG - HEAD_DIM - EXT), BF16)
        for k, hd in enumerate(range(g0, g0 + group)):
            qa_ref[hd] = jnp.concatenate([qb[:, hd * HEAD_DIM:(hd + 1) * HEAD_DIM],
                                          ext_t[:, k * EXT:(k + 1) * EXT], pad], axis=1)


def _q_aug(slopes, x, sc, sh, w, layer, kmbd, batch, seq):
    t, d = x.shape
    tm = MOBA_BLOCK
    n_blk = seq // tm
    rows = kmbd.shape[2]
    mod = lambda: pl.BlockSpec((None, 1, d), lambda i, sl: (i // n_blk, 0, 0))
    grid_spec = pltpu.PrefetchScalarGridSpec(
        num_scalar_prefetch=1,
        grid=(t // tm,),
        in_specs=[
            pl.BlockSpec((tm, d), lambda i, sl: (i, 0)),
            mod(), mod(),
            pl.BlockSpec((None, d, d), lambda i, sl: (layer, 0, 0)),
            pl.BlockSpec((None, 3, rows, d), lambda i, sl: (i // n_blk, 0, 0, 0)),
        ],
        out_specs=pl.BlockSpec((None, N_HEADS, tm, AUG), lambda i, sl: (i // n_blk, 0, i % n_blk, 0)),
    )
    return pl.pallas_call(
        functools.partial(_qaug_kernel, n_blk=n_blk),
        grid_spec=grid_spec,
        out_shape=jax.ShapeDtypeStruct((batch, N_HEADS, seq, AUG), BF16),
        compiler_params=_params(("arbitrary",)),
        name="q_aug",
    )(slopes, x, sc, sh, w, kmbd)


def _block_diag_means(km, n_blk):
    d = km.shape[-1]
    r = jnp.arange(N_HEADS * n_blk)
    head_of_col = jnp.arange(d) // HEAD_DIM
    bd = jnp.where((r // n_blk)[None, :, None] == head_of_col[None, None, :], km[:, r % n_blk, :], 0.0)
    return jnp.stack(_split3(bd), axis=1).astype(BF16)


def _attn_kernel(qlo_ref, qhi_ref, ka_ref, va_ref, olo_ref, ohi_ref, q_scr, s_scr, *, n_blk):
    i = pl.program_id(2)
    bq = MOBA_BLOCK
    half = n_blk // 2
    q_scr[0] = qlo_ref[...]
    q_scr[1] = qhi_ref[...]
    dmat = lax.broadcasted_iota(jnp.int32, (bq, bq), 1) - lax.broadcasted_iota(jnp.int32, (bq, bq), 0)

    def unit(t):
        if t >= half:
            return None, 1, pl.multiple_of((t - i - 1) * bq, bq)
        is_lo = i >= t
        key = jnp.where(is_lo, t, t - i - 1)
        return is_lo, jnp.where(is_lo, 0, 1), pl.multiple_of(key * bq, bq)

    neg = jnp.full((bq, LANES), NEG_INF, F32)
    mx = [[neg] * HEAD_PAIR, [neg] * HEAD_PAIR]
    for t in range(n_blk + 1):
        is_lo, sel, start = unit(t)
        if is_lo is not None:
            causal = dmat <= jnp.where(is_lo, i - t, n_blk - t) * bq
        elif t == n_blk:
            causal = dmat <= 0
        else:
            causal = None
        for hh in range(HEAD_PAIR):
            s = _dot_nt(q_scr[sel, hh], ka_ref[hh, pl.ds(start, bq), :])
            if causal is not None:
                s = jnp.where(causal, s, NEG_INF)
            s_scr[hh, t] = s
            smax = jnp.maximum(s[:, :LANES], s[:, LANES:])
            if is_lo is None:
                mx[1][hh] = jnp.maximum(mx[1][hh], smax)
            else:
                mx[0][hh] = jnp.maximum(mx[0][hh], jnp.where(is_lo, smax, NEG_INF))
                mx[1][hh] = jnp.maximum(mx[1][hh], jnp.where(is_lo, NEG_INF, smax))

    m_rows = [[jnp.broadcast_to(jnp.max(mx[sel][hh], axis=1, keepdims=True), (bq, LANES))
               for hh in range(HEAD_PAIR)] for sel in range(2)]
    zero = jnp.zeros((bq, AUG), F32)
    acc = [[zero] * HEAD_PAIR, [zero] * HEAD_PAIR]
    for t in range(n_blk + 1):
        is_lo, _, start = unit(t)
        for hh in range(HEAD_PAIR):
            m_row = m_rows[1][hh] if is_lo is None else jnp.where(is_lo, m_rows[0][hh], m_rows[1][hh])
            p = jnp.exp2(s_scr[hh, t] - jnp.concatenate([m_row, m_row], axis=1)).astype(BF16)
            pv = jnp.dot(p, va_ref[hh, pl.ds(start, bq), :], preferred_element_type=F32)
            if is_lo is None:
                acc[1][hh] = acc[1][hh] + pv
            else:
                acc[0][hh] = acc[0][hh] + jnp.where(is_lo, pv, 0.0)
                acc[1][hh] = acc[1][hh] + jnp.where(is_lo, 0.0, pv)

    for sel, o_ref in enumerate((olo_ref, ohi_ref)):
        outs = [a[:, :HEAD_DIM] / a[:, HEAD_DIM:HEAD_DIM + 1] for a in acc[sel]]
        o_ref[...] = jnp.concatenate(outs, axis=1).astype(o_ref.dtype)


def _attention(qa, ka, va, batch, seq):
    n_blk = seq // MOBA_BLOCK
    assert n_blk % 2 == 0
    pair = lambda: pl.BlockSpec((None, HEAD_PAIR, seq, AUG), lambda b, hp, i: (b, hp, 0, 0))
    half = n_blk // 2
    o_shape = jax.ShapeDtypeStruct((batch, seq // 2, D_MODEL), BF16)
    out_lo, out_hi = pl.pallas_call(
        functools.partial(_attn_kernel, n_blk=n_blk),
        grid=(batch, N_HEADS // HEAD_PAIR, n_blk // 2),
        in_specs=[
            pl.BlockSpec((None, HEAD_PAIR, MOBA_BLOCK, AUG), lambda b, hp, i: (b, hp, i, 0)),
            pl.BlockSpec((None, HEAD_PAIR, MOBA_BLOCK, AUG), lambda b, hp, i: (b, hp, n_blk - 1 - i, 0)),
            pair(), pair(),
        ],
        out_specs=[
            pl.BlockSpec((None, MOBA_BLOCK, HEAD_PAIR * HEAD_DIM), lambda b, hp, i: (b, i, hp)),
            pl.BlockSpec((None, MOBA_BLOCK, HEAD_PAIR * HEAD_DIM), lambda b, hp, i: (b, half - 1 - i, hp)),
        ],
        out_shape=[o_shape, o_shape],
        scratch_shapes=[
            pltpu.VMEM((2, HEAD_PAIR, MOBA_BLOCK, AUG), BF16),
            pltpu.VMEM((HEAD_PAIR, n_blk + 1, MOBA_BLOCK, MOBA_BLOCK), F32),
        ],
        compiler_params=_params(("arbitrary", "arbitrary", "arbitrary")),
        name="moba_attn",
    )(qa, qa, ka, va)
    return out_lo, out_hi


def kernel(x, c, ada_w, ada_b, ln_g, ln_b, conv_in_w, conv_in_b, conv_dw_w, conv_dw_b, conv_norm_g,
           conv_norm_b, conv_out_w, conv_out_b, kv_ada_w, kv_ada_b, w_kv, w_q, w_o, ffn_w13, ffn_w2,
           router_w, router_b, moe_w13, moe_w2):
    batch, seq, d = x.shape
    t = batch * seq
    assert d == D_MODEL and seq % ROW_TILE == 0 and seq % MOBA_BLOCK == 0 and t & (t - 1) == 0
    n_blk = seq // MOBA_BLOCK
    assert n_blk <= 16

    c_pad = jnp.pad(c, ((0, SUBLANES - batch), (0, 0)))
    mods = _ada(c_pad, ada_w, ada_b, 1536)[:, :batch]
    kv_mod = _ada(c_pad, kv_ada_w[None], kv_ada_b[None], 1024)[0, :batch]

    def mod_vec(v):
        return v.reshape(batch, 1, d)

    row = lambda v: v.reshape(1, -1)
    slopes = jnp.exp2(-8.0 * jnp.arange(1, N_HEADS + 1, dtype=F32) / N_HEADS)
    n_moe_tiles = 2 * t // MOE_TILE + N_EXPERTS
    conv_in_wb, conv_out_wb, w_qb, w_ob = [w.astype(BF16) for w in (conv_in_w, conv_out_w, w_q, w_o)]
    ffn_w13b, ffn_w2b, moe_w13b, moe_w2b = [w.astype(BF16) for w in (ffn_w13, ffn_w2, moe_w13, moe_w2)]

    xf = x.reshape(t, d)
    ka = va = kmbd = None
    for l in range(DEPTH):
        if l == N_A_LAYERS:
            ka, va, km = _shared_kv(xf, mod_vec(kv_mod[:, d:]), mod_vec(kv_mod[:, :d]),
                                    w_kv.astype(BF16), batch, seq)
            kmbd = _block_diag_means(km.reshape(batch, n_blk, d), n_blk)
        sh1, sc1, g1, sh2, sc2, g2 = [mod_vec(mods[l, :, k * d:(k + 1) * d]) for k in range(6)]
        lng1, lnb1 = row(ln_g[l, 0]), row(ln_b[l, 0])
        lng2, lnb2 = row(ln_g[l, 1]), row(ln_b[l, 1])
        if l < N_A_LAYERS:
            u = _conv_in(xf, sc1, sh1, conv_in_wb, l, row(conv_in_b[l]), seq)
            xf = _conv_block(u, xf, g1, conv_dw_w[l], row(conv_dw_b[l]), row(conv_norm_g[l]),
                             row(conv_norm_b[l]), conv_out_wb, l, row(conv_out_b[l]), lng1, lnb1, seq)
        else:
            jl = l - N_A_LAYERS
            qa = _q_aug(slopes, xf, sc1, sh1, w_qb, jl, kmbd, batch, seq)
            att_lo, att_hi = _attention(qa, ka, va, batch, seq)
            xf = _oproj(att_lo, att_hi, xf, g1, w_ob, jl, lng1, lnb1, seq)
        if l % 2 == 0:
            xf = _ffn(xf, sc2, sh2, g2, ffn_w13b, ffn_w2b, l // 2, lng2, lnb2, seq)
        else:
            e = l // 2
            h, meta_rows, meta_cols = _router(xf, sc2, sh2, router_w[e].T, router_b[e].reshape(-1, 1), seq)
            te, nvalid, row_dst = _route_rows(meta_rows[0].astype(jnp.int32), meta_rows[1].astype(jnp.int32),
                                              MOE_TILE, n_moe_tiles)
            y = _moe(h, te, nvalid, row_dst, moe_w13b, moe_w2b, e)
            xf = _combine(xf, y, meta_cols, g2, lng2, lnb2, seq)
    return xf.reshape(batch, seq, d)
```

```python
import functools

import jax
import jax.numpy as jnp
from jax import lax
from jax.experimental import pallas as pl
from jax.experimental.pallas import tpu as pltpu

F32 = jnp.float32
BF16 = jnp.bfloat16
HIGHEST = lax.Precision.HIGHEST

D_MODEL = 1024
DEPTH = 4
N_A_LAYERS = DEPTH // 2
CONV_K = 31
N_HEADS = 16
HEAD_DIM = D_MODEL // N_HEADS
MOBA_BLOCK = 256
MOBA_TOPK = 3
D_FF = 2816
N_EXPERTS = 8
D_FF_EXPERT = 3584
ALPHA = (2.0 * DEPTH) ** 0.25
LN_EPS = 1e-5
NEG_INF = -1e30

SUBLANES = 8
LANES = 128
VMEM_LIMIT = 48 * 1024 * 1024

ROW_TILE = 512
CONV_HALO = 32
CONV_TILE = 512
CONV_CHUNK = 32
FFN_CHUNK = 1408
MOE_CHUNK = 1792
MOE_TILE = 512
HEAD_PAIR = 2
AUG = 2 * HEAD_DIM
EXT = 32
LOG2E = 1.4426950408889634


def _params(sem, vmem=VMEM_LIMIT):
    return pltpu.CompilerParams(dimension_semantics=sem, vmem_limit_bytes=vmem)


def _layer_norm(y, g, b):
    mu = jnp.mean(y, axis=-1, keepdims=True)
    yc = y - mu
    var = jnp.mean(yc * yc, axis=-1, keepdims=True)
    return yc * lax.rsqrt(var + LN_EPS) * g + b


def _modulate(x, sc, sh):
    return x * (1.0 + sc) + sh


def _ada_kernel(c_ref, w_ref, b_ref, o_ref):
    ca = jax.nn.silu(c_ref[...])
    o_ref[...] = jnp.dot(ca, w_ref[...], precision=HIGHEST, preferred_element_type=F32) + b_ref[...]


def _ada(c_pad, w, b, tn):
    n_l, d, n = w.shape
    rows = c_pad.shape[0]
    return pl.pallas_call(
        _ada_kernel,
        grid=(n_l, n // tn),
        in_specs=[
            pl.BlockSpec((rows, d), lambda l, j: (0, 0)),
            pl.BlockSpec((None, d, tn), lambda l, j: (l, 0, j)),
            pl.BlockSpec((None, 1, tn), lambda l, j: (l, 0, j)),
        ],
        out_specs=pl.BlockSpec((None, rows, tn), lambda l, j: (l, 0, j)),
        out_shape=jax.ShapeDtypeStruct((n_l, rows, n), F32),
        compiler_params=_params(("arbitrary", "arbitrary")),
        name="ada",
    )(c_pad, w, b.reshape(n_l, 1, n))


def _conv_in_kernel(x_ref, sc_ref, sh_ref, w_ref, b_ref, o_ref):
    h = _modulate(x_ref[...], sc_ref[...], sh_ref[...]).astype(BF16)
    y = jnp.dot(h, w_ref[...], preferred_element_type=F32) + b_ref[...]
    half = y.shape[1] // 2
    o_ref[...] = y[:, :half] * jax.nn.sigmoid(y[:, half:])


def _conv_in(x, sc, sh, w, layer, b, seq):
    t, d = x.shape
    n = w.shape[2]
    n_out = n // 2
    tm = ROW_TILE
    per_seq = seq // tm
    return pl.pallas_call(
        _conv_in_kernel,
        grid=(t // tm,),
        in_specs=[
            pl.BlockSpec((tm, d), lambda i: (i, 0)),
            pl.BlockSpec((None, 1, d), lambda i: (i // per_seq, 0, 0)),
            pl.BlockSpec((None, 1, d), lambda i: (i // per_seq, 0, 0)),
            pl.BlockSpec((None, d, n), lambda i: (layer, 0, 0)),
            pl.BlockSpec((1, n), lambda i: (0, 0)),
        ],
        out_specs=pl.BlockSpec((tm, n_out), lambda i: (i, 0)),
        out_shape=jax.ShapeDtypeStruct((t, n_out), F32),
        compiler_params=_params(("arbitrary",)),
        name="conv_in",
    )(x, sc, sh, w, b)


def _conv_kernel(uprev_ref, u_ref, x_ref, g1_ref, wdw_ref, bdw_ref, gn_ref, bn_ref, wo_ref, bo_ref,
                 lng_ref, lnb_ref, o_ref, win_ref, cv_ref, *, per_seq):
    tm = u_ref.shape[0]
    rows = CONV_HALO + tm
    first = (pl.program_id(0) % per_seq) == 0
    win_ref[0, 0:CONV_HALO, :] = jnp.where(first, 0.0, uprev_ref[...])
    win_ref[0, CONV_HALO:, :] = u_ref[...]
    groups = rows // SUBLANES - 1
    cur = win_ref[0, SUBLANES:, :].reshape(groups, SUBLANES, D_MODEL)
    prev = win_ref[0, 0:rows - SUBLANES, :].reshape(groups, SUBLANES, D_MODEL)
    sub = lax.broadcasted_iota(jnp.int32, (groups, SUBLANES, D_MODEL), 1)
    for b in range(1, SUBLANES):
        mixed = jnp.where(sub < SUBLANES - b, cur, prev)
        win_ref[b, SUBLANES:, :] = pltpu.roll(mixed, b, axis=1).reshape(rows - SUBLANES, D_MODEL)
    for c in range(tm // CONV_CHUNK):
        base = CONV_HALO + c * CONV_CHUNK
        acc = jnp.zeros((CONV_CHUNK, D_MODEL), F32) + bdw_ref[...]
        for s in range(CONV_K):
            a, b = divmod(s, SUBLANES)
            k = CONV_K - 1 - s
            acc = acc + win_ref[b, base - SUBLANES * a:base - SUBLANES * a + CONV_CHUNK, :] * wdw_ref[k:k + 1, :]
        cv_ref[c * CONV_CHUNK:(c + 1) * CONV_CHUNK, :] = acc
    un = jax.nn.silu(_layer_norm(cv_ref[...], gn_ref[...], bn_ref[...])).astype(BF16)
    out = jnp.dot(un, wo_ref[...], preferred_element_type=F32) + bo_ref[...]
    y = ALPHA * x_ref[...] + (1.0 + g1_ref[...]) * out
    o_ref[...] = _layer_norm(y, lng_ref[...], lnb_ref[...])


def _conv_block(u, x, g1, wdw, bdw, gn, bn, wo, layer, bo, lng, lnb, seq):
    t, d = x.shape
    tm = CONV_TILE
    per_seq = seq // tm
    halo_per_tile = tm // CONV_HALO
    vec = lambda: pl.BlockSpec((1, d), lambda i: (0, 0))
    return pl.pallas_call(
        functools.partial(_conv_kernel, per_seq=per_seq),
        grid=(t // tm,),
        in_specs=[
            pl.BlockSpec((CONV_HALO, d), lambda i: (jnp.maximum(i * halo_per_tile - 1, 0), 0)),
            pl.BlockSpec((tm, d), lambda i: (i, 0)),
            pl.BlockSpec((tm, d), lambda i: (i, 0)),
            pl.BlockSpec((None, 1, d), lambda i: (i // per_seq, 0, 0)),
            pl.BlockSpec((CONV_K, d), lambda i: (0, 0)),
            vec(), vec(), vec(),
            pl.BlockSpec((None, d, d), lambda i: (layer, 0, 0)),
            vec(), vec(), vec(),
        ],
        out_specs=pl.BlockSpec((tm, d), lambda i: (i, 0)),
        out_shape=jax.ShapeDtypeStruct((t, d), F32),
        scratch_shapes=[pltpu.VMEM((SUBLANES, CONV_HALO + tm, d), F32), pltpu.VMEM((tm, d), F32)],
        compiler_params=_params(("arbitrary",)),
        name="conv_block",
    )(u, u, x, g1, wdw, bdw, gn, bn, wo, bo, lng, lnb)


def _oproj_kernel(alo_ref, ahi_ref, x_ref, g1_ref, w_ref, lng_ref, lnb_ref, o_ref, *, per_seq):
    first_half = (pl.program_id(0) % per_seq) < per_seq // 2
    a = jnp.where(first_half, alo_ref[...], ahi_ref[...])
    out = jnp.dot(a, w_ref[...], preferred_element_type=F32)
    y = ALPHA * x_ref[...] + (1.0 + g1_ref[...]) * out
    o_ref[...] = _layer_norm(y, lng_ref[...], lnb_ref[...])


def _oproj(a_lo, a_hi, x, g1, w, layer, lng, lnb, seq):
    t, d = x.shape
    tm = ROW_TILE
    per_seq = seq // tm
    half = per_seq // 2
    vec = lambda: pl.BlockSpec((1, d), lambda i: (0, 0))
    return pl.pallas_call(
        functools.partial(_oproj_kernel, per_seq=per_seq),
        grid=(t // tm,),
        in_specs=[
            pl.BlockSpec((None, tm, d), lambda i: (i // per_seq, jnp.minimum(i % per_seq, half - 1), 0)),
            pl.BlockSpec((None, tm, d), lambda i: (i // per_seq, jnp.maximum(i % per_seq - half, 0), 0)),
            pl.BlockSpec((tm, d), lambda i: (i, 0)),
            pl.BlockSpec((None, 1, d), lambda i: (i // per_seq, 0, 0)),
            pl.BlockSpec((None, d, d), lambda i: (layer, 0, 0)),
            vec(), vec(),
        ],
        out_specs=pl.BlockSpec((tm, d), lambda i: (i, 0)),
        out_shape=jax.ShapeDtypeStruct((t, d), F32),
        compiler_params=_params(("arbitrary",)),
        name="oproj",
    )(a_lo, a_hi, x, g1, w, lng, lnb)


def _swiglu_chunk(hb, wa_ref, wb_ref, w2_ref):
    a = jnp.dot(hb, wa_ref[...], preferred_element_type=F32)
    b = jnp.dot(hb, wb_ref[...], preferred_element_type=F32)
    g = (jax.nn.silu(a) * b).astype(BF16)
    return jnp.dot(g, w2_ref[...], preferred_element_type=F32)


def _ffn_kernel(x_ref, sc_ref, sh_ref, g2_ref, wa_ref, wb_ref, w2_ref, lng_ref, lnb_ref, o_ref,
                hb_ref, acc_ref):
    j = pl.program_id(1)

    @pl.when(j == 0)
    def _():
        hb_ref[...] = _modulate(x_ref[...], sc_ref[...], sh_ref[...]).astype(BF16)
        acc_ref[...] = jnp.zeros_like(acc_ref)

    acc_ref[...] += _swiglu_chunk(hb_ref[...], wa_ref, wb_ref, w2_ref)

    @pl.when(j == pl.num_programs(1) - 1)
    def _():
        y = ALPHA * x_ref[...] + (1.0 + g2_ref[...]) * acc_ref[...]
        o_ref[...] = _layer_norm(y, lng_ref[...], lnb_ref[...])


def _ffn(x, sc, sh, g2, w13, w2, layer, lng, lnb, seq):
    t, d = x.shape
    f = w2.shape[1]
    tm, fc = ROW_TILE, FFN_CHUNK
    n_fc = f // fc
    per_seq = seq // tm
    mod = lambda: pl.BlockSpec((None, 1, d), lambda i, j: (i // per_seq, 0, 0))
    vec = lambda: pl.BlockSpec((1, d), lambda i, j: (0, 0))
    return pl.pallas_call(
        _ffn_kernel,
        grid=(t // tm, n_fc),
        in_specs=[
            pl.BlockSpec((tm, d), lambda i, j: (i, 0)),
            mod(), mod(), mod(),
            pl.BlockSpec((None, d, fc), lambda i, j: (layer, 0, j)),
            pl.BlockSpec((None, d, fc), lambda i, j: (layer, 0, j + n_fc)),
            pl.BlockSpec((None, fc, d), lambda i, j: (layer, j, 0)),
            vec(), vec(),
        ],
        out_specs=pl.BlockSpec((tm, d), lambda i, j: (i, 0)),
        out_shape=jax.ShapeDtypeStruct((t, d), F32),
        scratch_shapes=[pltpu.VMEM((tm, d), BF16), pltpu.VMEM((tm, d), F32)],
        compiler_params=_params(("arbitrary", "arbitrary")),
        name="ffn",
    )(x, sc, sh, g2, w13, w13, w2, lng, lnb)


def _router_kernel(x_ref, sc_ref, sh_ref, rwt_ref, rb_ref, h_ref, rows_ref, cols_ref):
    h = _modulate(x_ref[...], sc_ref[...], sh_ref[...])
    h_ref[...] = h.reshape(h.shape[0], SUBLANES, LANES)
    lg = lax.dot_general(rwt_ref[...], h, (((1,), (1,)), ((), ())), precision=HIGHEST,
                         preferred_element_type=F32) + rb_ref[...]
    idx = lax.broadcasted_iota(jnp.int32, lg.shape, 0)
    m1 = jnp.max(lg, axis=0, keepdims=True)
    i1 = jnp.min(jnp.where(lg == m1, idx, N_EXPERTS), axis=0, keepdims=True)
    lg2 = jnp.where(idx == i1, -jnp.inf, lg)
    m2 = jnp.max(lg2, axis=0, keepdims=True)
    i2 = jnp.min(jnp.where(lg2 == m2, idx, N_EXPERTS), axis=0, keepdims=True)
    e2 = jnp.exp(m2 - m1)
    den = 1.0 + e2
    w1 = 1.0 / den
    w2 = e2 / den
    meta = jnp.where(idx == 0, i1.astype(F32),
                     jnp.where(idx == 1, i2.astype(F32),
                               jnp.where(idx == 2, w1, jnp.where(idx == 3, w2, 0.0))))
    rows_ref[...] = meta
    wide = jnp.concatenate([meta, jnp.zeros((LANES - N_EXPERTS, meta.shape[1]), F32)], axis=0)
    cols_ref[...] = wide.T


def _router(x, sc, sh, rwt, rb, seq):
    t, d = x.shape
    tm = ROW_TILE
    per_seq = seq // tm
    mod = lambda: pl.BlockSpec((None, 1, d), lambda i: (i // per_seq, 0, 0))
    return pl.pallas_call(
        _router_kernel,
        grid=(t // tm,),
        in_specs=[
            pl.BlockSpec((tm, d), lambda i: (i, 0)),
            mod(), mod(),
            pl.BlockSpec((N_EXPERTS, d), lambda i: (0, 0)),
            pl.BlockSpec((N_EXPERTS, 1), lambda i: (0, 0)),
        ],
        out_specs=[
            pl.BlockSpec((tm, SUBLANES, LANES), lambda i: (i, 0, 0)),
            pl.BlockSpec((N_EXPERTS, tm), lambda i: (0, i)),
            pl.BlockSpec((tm, LANES), lambda i: (i, 0)),
        ],
        out_shape=[
            jax.ShapeDtypeStruct((t, SUBLANES, LANES), F32),
            jax.ShapeDtypeStruct((N_EXPERTS, t), F32),
            jax.ShapeDtypeStruct((t, LANES), F32),
        ],
        compiler_params=_params(("arbitrary",)),
        name="router",
    )(x, sc, sh, rwt, rb)


def _moe_kernel(te_ref, nv_ref, tcur_ref, tnext_ref, dprev_ref, h_hbm, wa_ref, wb_ref, w2_ref, y_hbm,
                hbuf, hb_ref, obuf, gsem, ssem, *, n_fc):
    i = pl.program_id(0)
    j = pl.program_id(1)
    n_tiles = pl.num_programs(0)
    rows_per_step = hbuf.shape[1]
    d = D_MODEL
    tm = n_fc * rows_per_step
    slot = i % 2
    prev_slot = 1 - slot

    def gather_copy(tok, step, r):
        return pltpu.make_async_copy(h_hbm.at[pl.ds(tok, 1)], hbuf.at[step, pl.ds(r, 1)], gsem.at[0])

    def scatter_copy(step, r, dst):
        return pltpu.make_async_copy(obuf.at[prev_slot, step, pl.ds(r, 1)], y_hbm.at[pl.ds(dst, 1)], ssem.at[0])

    def wait_gather():
        pltpu.make_async_copy(hbuf, hbuf, gsem.at[0]).wait()

    def wait_scatter():
        pltpu.make_async_copy(obuf.at[0], obuf.at[0], ssem.at[0]).wait()

    @pl.when(j == 0)
    def _():
        @pl.when(i == 0)
        def _():
            obuf[...] = jnp.zeros_like(obuf)

            def body(row, carry):
                gather_copy(tcur_ref[0, row], row // rows_per_step, row % rows_per_step).start()
                return carry
            lax.fori_loop(0, tm, body, 0)

        wait_gather()
        hb_ref[...] = hbuf[...].reshape(tm, d).astype(BF16)

        @pl.when(i >= 1)
        def _():
            wait_scatter()

    def issue_row_dmas():
        base = j * rows_per_step
        for r in range(rows_per_step):
            gather_copy(tnext_ref[0, base + r], j, r).start()
            scatter_copy(j, r, dprev_ref[0, base + r]).start()

    used = nv_ref[i] > 0

    @pl.when(used)
    def _():
        contrib = _swiglu_chunk(hb_ref[...], wa_ref, wb_ref, w2_ref).reshape(n_fc, rows_per_step, SUBLANES, LANES)
        issue_row_dmas()

        @pl.when(j == 0)
        def _():
            obuf[slot] = contrib

        @pl.when(j > 0)
        def _():
            obuf[slot] += contrib

    @pl.when(jnp.logical_not(used))
    def _():
        issue_row_dmas()

    @pl.when((j == n_fc - 1) & (i == n_tiles - 1))
    def _():
        wait_gather()
        wait_scatter()


def _moe(h, tile_expert, tile_nvalid, row_dst, w13, w2, layer):
    t, d = h.shape[0], D_MODEL
    n_tiles, _, tm = row_dst.shape
    f = w2.shape[2]
    fc = MOE_CHUNK
    n_fc = f // fc
    lead = (2 * t + tm + jnp.arange(tm, dtype=jnp.int32)).reshape(1, 1, tm)
    prev_dst = jnp.concatenate([lead, row_dst[:-1]], axis=0)
    row_tok = row_dst & (t - 1)

    def chunk(i, j, nv):
        return jnp.where(nv[i] > 0, j, n_fc - 1)

    grid_spec = pltpu.PrefetchScalarGridSpec(
        num_scalar_prefetch=2,
        grid=(n_tiles, n_fc),
        in_specs=[
            pl.BlockSpec((None, 1, tm), lambda i, j, te, nv: (i, 0, 0), memory_space=pltpu.SMEM),
            pl.BlockSpec((None, 1, tm), lambda i, j, te, nv: (jnp.minimum(i + 1, n_tiles - 1), 0, 0),
                         memory_space=pltpu.SMEM),
            pl.BlockSpec((None, 1, tm), lambda i, j, te, nv: (i, 0, 0), memory_space=pltpu.SMEM),
            pl.BlockSpec(memory_space=pl.ANY),
            pl.BlockSpec((None, None, d, fc), lambda i, j, te, nv: (layer, te[i], 0, chunk(i, j, nv))),
            pl.BlockSpec((None, None, d, fc), lambda i, j, te, nv: (layer, te[i], 0, chunk(i, j, nv) + n_fc)),
            pl.BlockSpec((None, None, fc, d), lambda i, j, te, nv: (layer, te[i], chunk(i, j, nv), 0)),
        ],
        out_specs=pl.BlockSpec(memory_space=pl.ANY),
        scratch_shapes=[
            pltpu.VMEM((n_fc, tm // n_fc, SUBLANES, LANES), F32),
            pltpu.VMEM((tm, d), BF16),
            pltpu.VMEM((2, n_fc, tm // n_fc, SUBLANES, LANES), F32),
            pltpu.SemaphoreType.DMA((1,)),
            pltpu.SemaphoreType.DMA((1,)),
        ],
    )
    return pl.pallas_call(
        functools.partial(_moe_kernel, n_fc=n_fc),
        grid_spec=grid_spec,
        out_shape=jax.ShapeDtypeStruct((2 * t + 2 * tm, SUBLANES, LANES), F32),
        compiler_params=_params(("arbitrary", "arbitrary")),
        name="moe",
    )(tile_expert, tile_nvalid, row_tok, row_tok, prev_dst, h, w13, w13, w2)


def _route_rows(e1, e2, tm, n_tiles):
    t = e1.shape[0]
    experts = jnp.concatenate([e1, e2])
    order = jnp.argsort(experts, stable=True).astype(jnp.int32)
    counts = jnp.sum(experts[:, None] == jnp.arange(N_EXPERTS)[None, :], axis=0).astype(jnp.int32)
    tiles_per = (counts + tm - 1) // tm
    tile_end = jnp.cumsum(tiles_per)
    tile_start = tile_end - tiles_per
    group_start = jnp.cumsum(counts) - counts
    tile_ids = jnp.arange(n_tiles, dtype=jnp.int32)
    te = jnp.minimum(jnp.sum(tile_ids[:, None] >= tile_end[None, :], axis=1), N_EXPERTS - 1).astype(jnp.int32)
    used = tile_ids < tile_end[-1]
    first_row = (tile_ids - tile_start[te]) * tm
    nvalid = jnp.where(used, jnp.clip(counts[te] - first_row, 0, tm), 0).astype(jnp.int32)
    last_e = te[jnp.maximum(tile_end[-1] - 1, 0)]
    te = jnp.where(used, te, last_e)
    r = jnp.arange(tm, dtype=jnp.int32)
    src = group_start[te][:, None] + first_row[:, None] + r[None, :]
    valid = r[None, :] < nvalid[:, None]
    dump = 2 * t + (tile_ids % 2)[:, None] * tm + r[None, :]
    row_dst = jnp.where(valid, order[jnp.clip(src, 0, 2 * t - 1)], dump).astype(jnp.int32)
    return te, nvalid, row_dst.reshape(n_tiles, 1, tm)


def _combine_kernel(x_ref, y0_ref, y1_ref, gw_ref, g2_ref, lng_ref, lnb_ref, o_ref):
    gw = gw_ref[...]
    rows = x_ref.shape
    f = gw[:, 2:3] * y0_ref[...].reshape(rows) + gw[:, 3:4] * y1_ref[...].reshape(rows)
    y = ALPHA * x_ref[...] + (1.0 + g2_ref[...]) * f
    o_ref[...] = _layer_norm(y, lng_ref[...], lnb_ref[...])


def _combine(x, y, gw, g2, lng, lnb, seq):
    t, d = x.shape
    tm = ROW_TILE
    per_seq = seq // tm
    n_t = t // tm
    vec = lambda: pl.BlockSpec((1, d), lambda i: (0, 0))
    return pl.pallas_call(
        _combine_kernel,
        grid=(n_t,),
        in_specs=[
            pl.BlockSpec((tm, d), lambda i: (i, 0)),
            pl.BlockSpec((tm, SUBLANES, LANES), lambda i: (i, 0, 0)),
            pl.BlockSpec((tm, SUBLANES, LANES), lambda i: (i + n_t, 0, 0)),
            pl.BlockSpec((tm, LANES), lambda i: (i, 0)),
            pl.BlockSpec((None, 1, d), lambda i: (i // per_seq, 0, 0)),
            vec(), vec(),
        ],
        out_specs=pl.BlockSpec((tm, d), lambda i: (i, 0)),
        out_shape=jax.ShapeDtypeStruct((t, d), F32),
        compiler_params=_params(("arbitrary",)),
        name="combine",
    )(x, y, y, gw, g2, lng, lnb)


def _kv_kernel(x_ref, sc_ref, sh_ref, w_ref, ka_ref, va_ref, km_ref, *, blocks_per_seq):
    h = _modulate(x_ref[...], sc_ref[...], sh_ref[...]).astype(BF16)
    kv = jnp.dot(h, w_ref[...], preferred_element_type=F32)
    k = kv[:, :D_MODEL]
    v = kv[:, D_MODEL:]
    km_ref[...] = jnp.mean(k, axis=0, keepdims=True)
    j = pl.program_id(0) % blocks_per_seq
    shape = (MOBA_BLOCK, HEAD_DIM)
    col = lax.broadcasted_iota(jnp.int32, shape, 1)
    pos = lax.broadcasted_iota(jnp.int32, shape, 0).astype(F32)
    blk = (j * MOBA_BLOCK).astype(F32)
    kx = jnp.where(col < 3, pos,
                   jnp.where(col < 6, blk,
                             jnp.where((col >= 8) & (col < 11), 1.0,
                                       jnp.where(col == 16 + j, 1.0, 0.0)))).astype(BF16)
    vx = jnp.where(col == 0, 1.0, 0.0).astype(BF16)
    for hd in range(N_HEADS):
        lo = hd * HEAD_DIM
        ka_ref[hd] = jnp.concatenate([k[:, lo:lo + HEAD_DIM].astype(BF16), kx], axis=1)
        va_ref[hd] = jnp.concatenate([v[:, lo:lo + HEAD_DIM].astype(BF16), vx], axis=1)


def _shared_kv(x, sc, sh, w, batch, seq):
    t, d = x.shape
    tm = MOBA_BLOCK
    n_blk = seq // tm
    mod = lambda: pl.BlockSpec((None, 1, d), lambda i: (i // n_blk, 0, 0))
    aug = lambda: pl.BlockSpec((None, N_HEADS, tm, AUG), lambda i: (i // n_blk, 0, i % n_blk, 0))
    return pl.pallas_call(
        functools.partial(_kv_kernel, blocks_per_seq=n_blk),
        grid=(t // tm,),
        in_specs=[
            pl.BlockSpec((tm, d), lambda i: (i, 0)),
            mod(), mod(),
            pl.BlockSpec((d, 2 * d), lambda i: (0, 0)),
        ],
        out_specs=[aug(), aug(), pl.BlockSpec((None, None, 1, d), lambda i: (i // n_blk, i % n_blk, 0, 0))],
        out_shape=[
            jax.ShapeDtypeStruct((batch, N_HEADS, seq, AUG), BF16),
            jax.ShapeDtypeStruct((batch, N_HEADS, seq, AUG), BF16),
            jax.ShapeDtypeStruct((batch, n_blk, 1, d), F32),
        ],
        compiler_params=_params(("arbitrary",)),
        name="shared_kv",
    )(x, sc, sh, w)


def _split3(x):
    p1 = x.astype(BF16).astype(F32)
    p2 = (x - p1).astype(BF16).astype(F32)
    p3 = (x - p1 - p2).astype(BF16).astype(F32)
    return p1, p2, p3


def _dot_nt(a, b):
    return lax.dot_general(a, b, (((1,), (1,)), ((), ())), preferred_element_type=F32)


def _qaug_kernel(sl_ref, x_ref, sc_ref, sh_ref, w_ref, kmbd_ref, qa_ref, *, n_blk):
    i = pl.program_id(0) % n_blk
    bq = MOBA_BLOCK
    h = _modulate(x_ref[...], sc_ref[...], sh_ref[...]).astype(BF16)
    qb = (jnp.dot(h, w_ref[...], preferred_element_type=F32) * (HEAD_DIM ** -0.5 * LOG2E)).astype(BF16)
    gates = _dot_nt(kmbd_ref[0], qb) + _dot_nt(kmbd_ref[1], qb) + _dot_nt(kmbd_ref[2], qb)
    lane_q = lax.broadcasted_iota(jnp.int32, (SUBLANES, bq), 1).astype(F32)
    sub8 = lax.broadcasted_iota(jnp.int32, (SUBLANES, bq), 0)
    blk_id = lax.broadcasted_iota(jnp.int32, (n_blk, bq), 0)
    t_q = lane_q + (i * bq).astype(F32)
    group = LANES // EXT
    for g0 in range(0, N_HEADS, group):
        exts = []
        for hd in range(g0, g0 + group):
            gate = jnp.where(blk_id < i, gates[hd * n_blk:(hd + 1) * n_blk, :], -jnp.inf)
            rank = jnp.zeros((n_blk, bq), jnp.int32)
            for m in range(n_blk):
                gm = gate[m:m + 1, :]
                ahead = (gm > gate) | ((gm == gate) & (blk_id > m))
                rank = rank + ahead.astype(jnp.int32)
            bias = jnp.where((blk_id < i) & (rank >= MOBA_TOPK), NEG_INF, 0.0)
            slope = jnp.full((SUBLANES, bq), sl_ref[hd], F32) * LOG2E
            s1, s2, s3 = _split3(slope)
            t1, t2, t3 = _split3(-slope * t_q)
            rows_s = jnp.where((sub8 == 0) | (sub8 == 3), s1,
                               jnp.where((sub8 == 1) | (sub8 == 4), s2,
                                         jnp.where((sub8 == 2) | (sub8 == 5), s3, 0.0)))
            rows_t = jnp.where(sub8 == 0, t1, jnp.where(sub8 == 1, t2, jnp.where(sub8 == 2, t3, 0.0)))
            pieces = [rows_s, rows_t, bias]
            if EXT - 2 * SUBLANES - n_blk:
                pieces.append(jnp.zeros((EXT - 2 * SUBLANES - n_blk, bq), F32))
            exts.extend(pieces)
        ext_t = jnp.concatenate(exts, axis=0).T.astype(BF16)
        pad = jnp.zeros((bq, AUG - HEAD_DIM - EXT), BF16)
        for k, hd in enumerate(range(g0, g0 + group)):
            qa_ref[hd] = jnp.concatenate([qb[:, hd * HEAD_DIM:(hd + 1) * HEAD_DIM],
                                          ext_t[:, k * EXT:(k + 1) * EXT], pad], axis=1)


def _q_aug(slopes, x, sc, sh, w, layer, kmbd, batch, seq):
    t, d = x.shape
    tm = MOBA_BLOCK
    n_blk = seq // tm
    rows = kmbd.shape[2]
    mod = lambda: pl.BlockSpec((None, 1, d), lambda i, sl: (i // n_blk, 0, 0))
    grid_spec = pltpu.PrefetchScalarGridSpec(
        num_scalar_prefetch=1,
        grid=(t // tm,),
        in_specs=[
            pl.BlockSpec((tm, d), lambda i, sl: (i, 0)),
            mod(), mod(),
            pl.BlockSpec((None, d, d), lambda i, sl: (layer, 0, 0)),
            pl.BlockSpec((None, 3, rows, d), lambda i, sl: (i // n_blk, 0, 0, 0)),
        ],
        out_specs=pl.BlockSpec((None, N_HEADS, tm, AUG), lambda i, sl: (i // n_blk, 0, i % n_blk, 0)),
    )
    return pl.pallas_call(
        functools.partial(_qaug_kernel, n_blk=n_blk),
        grid_spec=grid_spec,
        out_shape=jax.ShapeDtypeStruct((batch, N_HEADS, seq, AUG), BF16),
        compiler_params=_params(("arbitrary",)),
        name="q_aug",
    )(slopes, x, sc, sh, w, kmbd)


def _block_diag_means(km, n_blk):
    d = km.shape[-1]
    r = jnp.arange(N_HEADS * n_blk)
    head_of_col = jnp.arange(d) // HEAD_DIM
    bd = jnp.where((r // n_blk)[None, :, None] == head_of_col[None, None, :], km[:, r % n_blk, :], 0.0)
    return jnp.stack(_split3(bd), axis=1).astype(BF16)


def _attn_kernel(qlo_ref, qhi_ref, ka_ref, va_ref, olo_ref, ohi_ref, q_scr, s_scr, *, n_blk):
    i = pl.program_id(2)
    bq = MOBA_BLOCK
    half = n_blk // 2
    q_scr[0] = qlo_ref[...]
    q_scr[1] = qhi_ref[...]
    dmat = lax.broadcasted_iota(jnp.int32, (bq, bq), 1) - lax.broadcasted_iota(jnp.int32, (bq, bq), 0)

    def unit(t):
        if t >= half:
            return None, 1, pl.multiple_of((t - i - 1) * bq, bq)
        is_lo = i >= t
        key = jnp.where(is_lo, t, t - i - 1)
        return is_lo, jnp.where(is_lo, 0, 1), pl.multiple_of(key * bq, bq)

    neg = jnp.full((bq, LANES), NEG_INF, F32)
    zero = jnp.zeros((bq, AUG), F32)
    mx = [[neg] * HEAD_PAIR, [neg] * HEAD_PAIR]
    acc = [[zero] * HEAD_PAIR, [zero] * HEAD_PAIR]
    m_rows = [[None] * HEAD_PAIR, [None] * HEAD_PAIR]

    def score_unit(t, hh):
        is_lo, sel, start = unit(t)
        s = _dot_nt(q_scr[sel, hh], ka_ref[hh, pl.ds(start, bq), :])
        if is_lo is not None:
            s = jnp.where(dmat <= jnp.where(is_lo, i - t, n_blk - t) * bq, s, NEG_INF)
        elif t == n_blk:
            s = jnp.where(dmat <= 0, s, NEG_INF)
        s_scr[hh, t] = s
        smax = jnp.maximum(s[:, :LANES], s[:, LANES:])
        if is_lo is None:
            mx[1][hh] = jnp.maximum(mx[1][hh], smax)
        else:
            mx[0][hh] = jnp.maximum(mx[0][hh], jnp.where(is_lo, smax, NEG_INF))
            mx[1][hh] = jnp.maximum(mx[1][hh], jnp.where(is_lo, NEG_INF, smax))

    def finish_max(hh):
        for sel in range(2):
            m_rows[sel][hh] = jnp.broadcast_to(jnp.max(mx[sel][hh], axis=1, keepdims=True), (bq, LANES))

    def value_unit(t, hh):
        is_lo, _, start = unit(t)
        m_row = m_rows[1][hh] if is_lo is None else jnp.where(is_lo, m_rows[0][hh], m_rows[1][hh])
        p = jnp.exp2(s_scr[hh, t] - jnp.concatenate([m_row, m_row], axis=1)).astype(BF16)
        pv = jnp.dot(p, va_ref[hh, pl.ds(start, bq), :], preferred_element_type=F32)
        if is_lo is None:
            acc[1][hh] = acc[1][hh] + pv
        else:
            acc[0][hh] = acc[0][hh] + jnp.where(is_lo, pv, 0.0)
            acc[1][hh] = acc[1][hh] + jnp.where(is_lo, 0.0, pv)

    units = range(n_blk + 1)
    for t in units:
        score_unit(t, 0)
    for hh in range(HEAD_PAIR):
        finish_max(hh)
        for t in units:
            if hh + 1 < HEAD_PAIR:
                score_unit(t, hh + 1)
            value_unit(t, hh)

    for sel, o_ref in enumerate((olo_ref, ohi_ref)):
        outs = [a[:, :HEAD_DIM] / a[:, HEAD_DIM:HEAD_DIM + 1] for a in acc[sel]]
        o_ref[...] = jnp.concatenate(outs, axis=1).astype(o_ref.dtype)


def _attention(qa, ka, va, batch, seq):
    n_blk = seq // MOBA_BLOCK
    assert n_blk % 2 == 0
    pair = lambda: pl.BlockSpec((None, HEAD_PAIR, seq, AUG), lambda b, hp, i: (b, hp, 0, 0))
    half = n_blk // 2
    o_shape = jax.ShapeDtypeStruct((batch, seq // 2, D_MODEL), BF16)
    out_lo, out_hi = pl.pallas_call(
        functools.partial(_attn_kernel, n_blk=n_blk),
        grid=(batch, N_HEADS // HEAD_PAIR, n_blk // 2),
        in_specs=[
            pl.BlockSpec((None, HEAD_PAIR, MOBA_BLOCK, AUG), lambda b, hp, i: (b, hp, i, 0)),
            pl.BlockSpec((None, HEAD_PAIR, MOBA_BLOCK, AUG), lambda b, hp, i: (b, hp, n_blk - 1 - i, 0)),
            pair(), pair(),
        ],
        out_specs=[
            pl.BlockSpec((None, MOBA_BLOCK, HEAD_PAIR * HEAD_DIM), lambda b, hp, i: (b, i, hp)),
            pl.BlockSpec((None, MOBA_BLOCK, HEAD_PAIR * HEAD_DIM), lambda b, hp, i: (b, half - 1 - i, hp)),
        ],
        out_shape=[o_shape, o_shape],
        scratch_shapes=[
            pltpu.VMEM((2, HEAD_PAIR, MOBA_BLOCK, AUG), BF16),
            pltpu.VMEM((HEAD_PAIR, n_blk + 1, MOBA_BLOCK, MOBA_BLOCK), F32),
        ],
        compiler_params=_params(("arbitrary", "arbitrary", "arbitrary")),
        name="moba_attn",
    )(qa, qa, ka, va)
    return out_lo, out_hi


def kernel(x, c, ada_w, ada_b, ln_g, ln_b, conv_in_w, conv_in_b, conv_dw_w, conv_dw_b, conv_norm_g,
           conv_norm_b, conv_out_w, conv_out_b, kv_ada_w, kv_ada_b, w_kv, w_q, w_o, ffn_w13, ffn_w2,
           router_w, router_b, moe_w13, moe_w2):
    batch, seq, d = x.shape
    t = batch * seq
    assert d == D_MODEL and seq % ROW_TILE == 0 and seq % MOBA_BLOCK == 0 and t & (t - 1) == 0
    n_blk = seq // MOBA_BLOCK
    assert n_blk <= 16

    c_pad = jnp.pad(c, ((0, SUBLANES - batch), (0, 0)))
    mods = _ada(c_pad, ada_w, ada_b, 1536)[:, :batch]
    kv_mod = _ada(c_pad, kv_ada_w[None], kv_ada_b[None], 1024)[0, :batch]

    def mod_vec(v):
        return v.reshape(batch, 1, d)

    row = lambda v: v.reshape(1, -1)
    slopes = jnp.exp2(-8.0 * jnp.arange(1, N_HEADS + 1, dtype=F32) / N_HEADS)
    n_moe_tiles = 2 * t // MOE_TILE + N_EXPERTS
    conv_in_wb, conv_out_wb, w_qb, w_ob = [w.astype(BF16) for w in (conv_in_w, conv_out_w, w_q, w_o)]
    ffn_w13b, ffn_w2b, moe_w13b, moe_w2b = [w.astype(BF16) for w in (ffn_w13, ffn_w2, moe_w13, moe_w2)]

    xf = x.reshape(t, d)
    ka = va = kmbd = None
    for l in range(DEPTH):
        if l == N_A_LAYERS:
            ka, va, km = _shared_kv(xf, mod_vec(kv_mod[:, d:]), mod_vec(kv_mod[:, :d]),
                                    w_kv.astype(BF16), batch, seq)
            kmbd = _block_diag_means(km.reshape(batch, n_blk, d), n_blk)
        sh1, sc1, g1, sh2, sc2, g2 = [mod_vec(mods[l, :, k * d:(k + 1) * d]) for k in range(6)]
        lng1, lnb1 = row(ln_g[l, 0]), row(ln_b[l, 0])
        lng2, lnb2 = row(ln_g[l, 1]), row(ln_b[l, 1])
        if l < N_A_LAYERS:
            u = _conv_in(xf, sc1, sh1, conv_in_wb, l, row(conv_in_b[l]), seq)
            xf = _conv_block(u, xf, g1, conv_dw_w[l], row(conv_dw_b[l]), row(conv_norm_g[l]),
                             row(conv_norm_b[l]), conv_out_wb, l, row(conv_out_b[l]), lng1, lnb1, seq)
        else:
            jl = l - N_A_LAYERS
            qa = _q_aug(slopes, xf, sc1, sh1, w_qb, jl, kmbd, batch, seq)
            att_lo, att_hi = _attention(qa, ka, va, batch, seq)
            xf = _oproj(att_lo, att_hi, xf, g1, w_ob, jl, lng1, lnb1, seq)
        if l % 2 == 0:
            xf = _ffn(xf, sc2, sh2, g2, ffn_w13b, ffn_w2b, l // 2, lng2, lnb2, seq)
        else:
            e = l // 2
            h, meta_rows, meta_cols = _router(xf, sc2, sh2, router_w[e].T, router_b[e].reshape(-1, 1), seq)
            te, nvalid, row_dst = _route_rows(meta_rows[0].astype(jnp.int32), meta_rows[1].astype(jnp.int32),
                                              MOE_TILE, n_moe_tiles)
            y = _moe(h, te, nvalid, row_dst, moe_w13b, moe_w2b, e)
            xf = _combine(xf, y, meta_cols, g2, lng2, lnb2, seq)
    return xf.reshape(batch, seq, d)
```

```python
import functools

import jax
import jax.numpy as jnp
from jax import lax
from jax.experimental import pallas as pl
from jax.experimental.pallas import tpu as pltpu

F32 = jnp.float32
BF16 = jnp.bfloat16
HIGHEST = lax.Precision.HIGHEST

D_MODEL = 1024
DEPTH = 4
N_A_LAYERS = DEPTH // 2
CONV_K = 31
N_HEADS = 16
HEAD_DIM = D_MODEL // N_HEADS
MOBA_BLOCK = 256
MOBA_TOPK = 3
D_FF = 2816
N_EXPERTS = 8
D_FF_EXPERT = 3584
ALPHA = (2.0 * DEPTH) ** 0.25
LN_EPS = 1e-5
NEG_INF = -1e30

SUBLANES = 8
LANES = 128
VMEM_LIMIT = 48 * 1024 * 1024

ROW_TILE = 512
CONV_HALO = 32
CONV_TILE = 512
CONV_CHUNK = 32
FFN_CHUNK = 1408
MOE_CHUNK = 1792
MOE_TILE = 512
HEAD_PAIR = 2
AUG = 2 * HEAD_DIM
EXT = 32
LOG2E = 1.4426950408889634


def _params(sem, vmem=VMEM_LIMIT):
    return pltpu.CompilerParams(dimension_semantics=sem, vmem_limit_bytes=vmem)


def _layer_norm(y, g, b):
    mu = jnp.mean(y, axis=-1, keepdims=True)
    yc = y - mu
    var = jnp.mean(yc * yc, axis=-1, keepdims=True)
    return yc * lax.rsqrt(var + LN_EPS) * g + b


def _modulate(x, sc, sh):
    return x * (1.0 + sc) + sh


def _ada_kernel(c_ref, w_ref, b_ref, o_ref):
    ca = jax.nn.silu(c_ref[...])
    o_ref[...] = jnp.dot(ca, w_ref[...], precision=HIGHEST, preferred_element_type=F32) + b_ref[...]


def _ada(c_pad, w, b, tn):
    n_l, d, n = w.shape
    rows = c_pad.shape[0]
    return pl.pallas_call(
        _ada_kernel,
        grid=(n_l, n // tn),
        in_specs=[
            pl.BlockSpec((rows, d), lambda l, j: (0, 0)),
            pl.BlockSpec((None, d, tn), lambda l, j: (l, 0, j)),
            pl.BlockSpec((None, 1, tn), lambda l, j: (l, 0, j)),
        ],
        out_specs=pl.BlockSpec((None, rows, tn), lambda l, j: (l, 0, j)),
        out_shape=jax.ShapeDtypeStruct((n_l, rows, n), F32),
        compiler_params=_params(("arbitrary", "arbitrary")),
        name="ada",
    )(c_pad, w, b.reshape(n_l, 1, n))


def _conv_in_kernel(x_ref, sc_ref, sh_ref, w_ref, b_ref, o_ref):
    h = _modulate(x_ref[...], sc_ref[...], sh_ref[...]).astype(BF16)
    y = jnp.dot(h, w_ref[...], preferred_element_type=F32) + b_ref[...]
    half = y.shape[1] // 2
    o_ref[...] = y[:, :half] * jax.nn.sigmoid(y[:, half:])


def _conv_in(x, sc, sh, w, layer, b, seq):
    t, d = x.shape
    n = w.shape[2]
    n_out = n // 2
    tm = ROW_TILE
    per_seq = seq // tm
    return pl.pallas_call(
        _conv_in_kernel,
        grid=(t // tm,),
        in_specs=[
            pl.BlockSpec((tm, d), lambda i: (i, 0)),
            pl.BlockSpec((None, 1, d), lambda i: (i // per_seq, 0, 0)),
            pl.BlockSpec((None, 1, d), lambda i: (i // per_seq, 0, 0)),
            pl.BlockSpec((None, d, n), lambda i: (layer, 0, 0)),
            pl.BlockSpec((1, n), lambda i: (0, 0)),
        ],
        out_specs=pl.BlockSpec((tm, n_out), lambda i: (i, 0)),
        out_shape=jax.ShapeDtypeStruct((t, n_out), F32),
        compiler_params=_params(("arbitrary",)),
        name="conv_in",
    )(x, sc, sh, w, b)


def _conv_kernel(uprev_ref, u_ref, x_ref, g1_ref, wdw_ref, bdw_ref, gn_ref, bn_ref, wo_ref, bo_ref,
                 lng_ref, lnb_ref, o_ref, win_ref, cv_ref, *, per_seq):
    tm = u_ref.shape[0]
    rows = CONV_HALO + tm
    first = (pl.program_id(0) % per_seq) == 0
    win_ref[0, 0:CONV_HALO, :] = jnp.where(first, 0.0, uprev_ref[...])
    win_ref[0, CONV_HALO:, :] = u_ref[...]
    groups = rows // SUBLANES - 1
    cur = win_ref[0, SUBLANES:, :].reshape(groups, SUBLANES, D_MODEL)
    prev = win_ref[0, 0:rows - SUBLANES, :].reshape(groups, SUBLANES, D_MODEL)
    sub = lax.broadcasted_iota(jnp.int32, (groups, SUBLANES, D_MODEL), 1)
    for b in range(1, SUBLANES):
        mixed = jnp.where(sub < SUBLANES - b, cur, prev)
        win_ref[b, SUBLANES:, :] = pltpu.roll(mixed, b, axis=1).reshape(rows - SUBLANES, D_MODEL)
    for c in range(tm // CONV_CHUNK):
        base = CONV_HALO + c * CONV_CHUNK
        acc = jnp.zeros((CONV_CHUNK, D_MODEL), F32) + bdw_ref[...]
        for s in range(CONV_K):
            a, b = divmod(s, SUBLANES)
            k = CONV_K - 1 - s
            acc = acc + win_ref[b, base - SUBLANES * a:base - SUBLANES * a + CONV_CHUNK, :] * wdw_ref[k:k + 1, :]
        cv_ref[c * CONV_CHUNK:(c + 1) * CONV_CHUNK, :] = acc
    un = jax.nn.silu(_layer_norm(cv_ref[...], gn_ref[...], bn_ref[...])).astype(BF16)
    out = jnp.dot(un, wo_ref[...], preferred_element_type=F32) + bo_ref[...]
    y = ALPHA * x_ref[...] + (1.0 + g1_ref[...]) * out
    o_ref[...] = _layer_norm(y, lng_ref[...], lnb_ref[...])


def _conv_block(u, x, g1, wdw, bdw, gn, bn, wo, layer, bo, lng, lnb, seq):
    t, d = x.shape
    tm = CONV_TILE
    per_seq = seq // tm
    halo_per_tile = tm // CONV_HALO
    vec = lambda: pl.BlockSpec((1, d), lambda i: (0, 0))
    return pl.pallas_call(
        functools.partial(_conv_kernel, per_seq=per_seq),
        grid=(t // tm,),
        in_specs=[
            pl.BlockSpec((CONV_HALO, d), lambda i: (jnp.maximum(i * halo_per_tile - 1, 0), 0)),
            pl.BlockSpec((tm, d), lambda i: (i, 0)),
            pl.BlockSpec((tm, d), lambda i: (i, 0)),
            pl.BlockSpec((None, 1, d), lambda i: (i // per_seq, 0, 0)),
            pl.BlockSpec((CONV_K, d), lambda i: (0, 0)),
            vec(), vec(), vec(),
            pl.BlockSpec((None, d, d), lambda i: (layer, 0, 0)),
            vec(), vec(), vec(),
        ],
        out_specs=pl.BlockSpec((tm, d), lambda i: (i, 0)),
        out_shape=jax.ShapeDtypeStruct((t, d), F32),
        scratch_shapes=[pltpu.VMEM((SUBLANES, CONV_HALO + tm, d), F32), pltpu.VMEM((tm, d), F32)],
        compiler_params=_params(("arbitrary",)),
        name="conv_block",
    )(u, u, x, g1, wdw, bdw, gn, bn, wo, bo, lng, lnb)


def _oproj_kernel(alo_ref, ahi_ref, x_ref, g1_ref, w_ref, lng_ref, lnb_ref, o_ref, *, per_seq):
    first_half = (pl.program_id(0) % per_seq) < per_seq // 2
    a = jnp.where(first_half, alo_ref[...], ahi_ref[...])
    out = jnp.dot(a, w_ref[...], preferred_element_type=F32)
    y = ALPHA * x_ref[...] + (1.0 + g1_ref[...]) * out
    o_ref[...] = _layer_norm(y, lng_ref[...], lnb_ref[...])


def _oproj(a_lo, a_hi, x, g1, w, layer, lng, lnb, seq):
    t, d = x.shape
    tm = ROW_TILE
    per_seq = seq // tm
    half = per_seq // 2
    vec = lambda: pl.BlockSpec((1, d), lambda i: (0, 0))
    return pl.pallas_call(
        functools.partial(_oproj_kernel, per_seq=per_seq),
        grid=(t // tm,),
        in_specs=[
            pl.BlockSpec((None, tm, d), lambda i: (i // per_seq, jnp.minimum(i % per_seq, half - 1), 0)),
            pl.BlockSpec((None, tm, d), lambda i: (i // per_seq, jnp.maximum(i % per_seq - half, 0), 0)),
            pl.BlockSpec((tm, d), lambda i: (i, 0)),
            pl.BlockSpec((None, 1, d), lambda i: (i // per_seq, 0, 0)),
            pl.BlockSpec((None, d, d), lambda i: (layer, 0, 0)),
            vec(), vec(),
        ],
        out_specs=pl.BlockSpec((tm, d), lambda i: (i, 0)),
        out_shape=jax.ShapeDtypeStruct((t, d), F32),
        compiler_params=_params(("arbitrary",)),
        name="oproj",
    )(a_lo, a_hi, x, g1, w, lng, lnb)


def _swiglu_chunk(hb, wa_ref, wb_ref, w2_ref):
    a = jnp.dot(hb, wa_ref[...], preferred_element_type=F32)
    b = jnp.dot(hb, wb_ref[...], preferred_element_type=F32)
    g = (jax.nn.silu(a) * b).astype(BF16)
    return jnp.dot(g, w2_ref[...], preferred_element_type=F32)


def _ffn_kernel(x_ref, sc_ref, sh_ref, g2_ref, wa_ref, wb_ref, w2_ref, lng_ref, lnb_ref, o_ref,
                hb_ref, acc_ref):
    j = pl.program_id(1)

    @pl.when(j == 0)
    def _():
        hb_ref[...] = _modulate(x_ref[...], sc_ref[...], sh_ref[...]).astype(BF16)
        acc_ref[...] = jnp.zeros_like(acc_ref)

    acc_ref[...] += _swiglu_chunk(hb_ref[...], wa_ref, wb_ref, w2_ref)

    @pl.when(j == pl.num_programs(1) - 1)
    def _():
        y = ALPHA * x_ref[...] + (1.0 + g2_ref[...]) * acc_ref[...]
        o_ref[...] = _layer_norm(y, lng_ref[...], lnb_ref[...])


def _ffn(x, sc, sh, g2, w13, w2, layer, lng, lnb, seq):
    t, d = x.shape
    f = w2.shape[1]
    tm, fc = ROW_TILE, FFN_CHUNK
    n_fc = f // fc
    per_seq = seq // tm
    mod = lambda: pl.BlockSpec((None, 1, d), lambda i, j: (i // per_seq, 0, 0))
    vec = lambda: pl.BlockSpec((1, d), lambda i, j: (0, 0))
    return pl.pallas_call(
        _ffn_kernel,
        grid=(t // tm, n_fc),
        in_specs=[
            pl.BlockSpec((tm, d), lambda i, j: (i, 0)),
            mod(), mod(), mod(),
            pl.BlockSpec((None, d, fc), lambda i, j: (layer, 0, j)),
            pl.BlockSpec((None, d, fc), lambda i, j: (layer, 0, j + n_fc)),
            pl.BlockSpec((None, fc, d), lambda i, j: (layer, j, 0)),
            vec(), vec(),
        ],
        out_specs=pl.BlockSpec((tm, d), lambda i, j: (i, 0)),
        out_shape=jax.ShapeDtypeStruct((t, d), F32),
        scratch_shapes=[pltpu.VMEM((tm, d), BF16), pltpu.VMEM((tm, d), F32)],
        compiler_params=_params(("arbitrary", "arbitrary")),
        name="ffn",
    )(x, sc, sh, g2, w13, w13, w2, lng, lnb)


def _router_kernel(x_ref, sc_ref, sh_ref, rwt_ref, rb_ref, h_ref, rows_ref, cols_ref):
    h = _modulate(x_ref[...], sc_ref[...], sh_ref[...])
    h_ref[...] = h.reshape(h.shape[0], SUBLANES, LANES)
    lg = lax.dot_general(rwt_ref[...], h, (((1,), (1,)), ((), ())), precision=HIGHEST,
                         preferred_element_type=F32) + rb_ref[...]
    idx = lax.broadcasted_iota(jnp.int32, lg.shape, 0)
    m1 = jnp.max(lg, axis=0, keepdims=True)
    i1 = jnp.min(jnp.where(lg == m1, idx, N_EXPERTS), axis=0, keepdims=True)
    lg2 = jnp.where(idx == i1, -jnp.inf, lg)
    m2 = jnp.max(lg2, axis=0, keepdims=True)
    i2 = jnp.min(jnp.where(lg2 == m2, idx, N_EXPERTS), axis=0, keepdims=True)
    e2 = jnp.exp(m2 - m1)
    den = 1.0 + e2
    w1 = 1.0 / den
    w2 = e2 / den
    meta = jnp.where(idx == 0, i1.astype(F32),
                     jnp.where(idx == 1, i2.astype(F32),
                               jnp.where(idx == 2, w1, jnp.where(idx == 3, w2, 0.0))))
    rows_ref[...] = meta
    wide = jnp.concatenate([meta, jnp.zeros((LANES - N_EXPERTS, meta.shape[1]), F32)], axis=0)
    cols_ref[...] = wide.T


def _router(x, sc, sh, rwt, rb, seq):
    t, d = x.shape
    tm = ROW_TILE
    per_seq = seq // tm
    mod = lambda: pl.BlockSpec((None, 1, d), lambda i: (i // per_seq, 0, 0))
    return pl.pallas_call(
        _router_kernel,
        grid=(t // tm,),
        in_specs=[
            pl.BlockSpec((tm, d), lambda i: (i, 0)),
            mod(), mod(),
            pl.BlockSpec((N_EXPERTS, d), lambda i: (0, 0)),
            pl.BlockSpec((N_EXPERTS, 1), lambda i: (0, 0)),
        ],
        out_specs=[
            pl.BlockSpec((tm, SUBLANES, LANES), lambda i: (i, 0, 0)),
            pl.BlockSpec((N_EXPERTS, tm), lambda i: (0, i)),
            pl.BlockSpec((tm, LANES), lambda i: (i, 0)),
        ],
        out_shape=[
            jax.ShapeDtypeStruct((t, SUBLANES, LANES), F32),
            jax.ShapeDtypeStruct((N_EXPERTS, t), F32),
            jax.ShapeDtypeStruct((t, LANES), F32),
        ],
        compiler_params=_params(("arbitrary",)),
        name="router",
    )(x, sc, sh, rwt, rb)


def _moe_kernel(te_ref, nv_ref, tcur_ref, tnext_ref, dprev_ref, h_hbm, wa_ref, wb_ref, w2_ref, y_hbm,
                hbuf, hb_ref, obuf, gsem, ssem, *, n_fc):
    i = pl.program_id(0)
    j = pl.program_id(1)
    n_tiles = pl.num_programs(0)
    rows_per_step = hbuf.shape[1]
    d = D_MODEL
    tm = n_fc * rows_per_step
    slot = i % 2
    prev_slot = 1 - slot

    def gather_copy(tok, step, r):
        return pltpu.make_async_copy(h_hbm.at[pl.ds(tok, 1)], hbuf.at[step, pl.ds(r, 1)], gsem.at[0])

    def scatter_copy(step, r, dst):
        return pltpu.make_async_copy(obuf.at[prev_slot, step, pl.ds(r, 1)], y_hbm.at[pl.ds(dst, 1)], ssem.at[0])

    def wait_gather():
        pltpu.make_async_copy(hbuf, hbuf, gsem.at[0]).wait()

    def wait_scatter():
        pltpu.make_async_copy(obuf.at[0], obuf.at[0], ssem.at[0]).wait()

    @pl.when(j == 0)
    def _():
        @pl.when(i == 0)
        def _():
            obuf[...] = jnp.zeros_like(obuf)

            def body(row, carry):
                gather_copy(tcur_ref[0, row], row // rows_per_step, row % rows_per_step).start()
                return carry
            lax.fori_loop(0, tm, body, 0)

        wait_gather()
        hb_ref[...] = hbuf[...].reshape(tm, d).astype(BF16)

        @pl.when(i >= 1)
        def _():
            wait_scatter()

    def issue_row_dmas():
        base = j * rows_per_step
        for r in range(rows_per_step):
            gather_copy(tnext_ref[0, base + r], j, r).start(priority=r % 2)
            scatter_copy(j, r, dprev_ref[0, base + r]).start(priority=r % 2)

    used = nv_ref[i] > 0

    @pl.when(used)
    def _():
        contrib = _swiglu_chunk(hb_ref[...], wa_ref, wb_ref, w2_ref).reshape(n_fc, rows_per_step, SUBLANES, LANES)
        issue_row_dmas()

        @pl.when(j == 0)
        def _():
            obuf[slot] = contrib

        @pl.when(j > 0)
        def _():
            obuf[slot] += contrib

    @pl.when(jnp.logical_not(used))
    def _():
        issue_row_dmas()

    @pl.when((j == n_fc - 1) & (i == n_tiles - 1))
    def _():
        wait_gather()
        wait_scatter()


def _moe(h, tile_expert, tile_nvalid, row_dst, w13, w2, layer):
    t, d = h.shape[0], D_MODEL
    n_tiles, _, tm = row_dst.shape
    f = w2.shape[2]
    fc = MOE_CHUNK
    n_fc = f // fc
    lead = (2 * t + tm + jnp.arange(tm, dtype=jnp.int32)).reshape(1, 1, tm)
    prev_dst = jnp.concatenate([lead, row_dst[:-1]], axis=0)
    row_tok = row_dst & (t - 1)

    def chunk(i, j, nv):
        return jnp.where(nv[i] > 0, j, n_fc - 1)

    grid_spec = pltpu.PrefetchScalarGridSpec(
        num_scalar_prefetch=2,
        grid=(n_tiles, n_fc),
        in_specs=[
            pl.BlockSpec((None, 1, tm), lambda i, j, te, nv: (i, 0, 0), memory_space=pltpu.SMEM),
            pl.BlockSpec((None, 1, tm), lambda i, j, te, nv: (jnp.minimum(i + 1, n_tiles - 1), 0, 0),
                         memory_space=pltpu.SMEM),
            pl.BlockSpec((None, 1, tm), lambda i, j, te, nv: (i, 0, 0), memory_space=pltpu.SMEM),
            pl.BlockSpec(memory_space=pl.ANY),
            pl.BlockSpec((None, None, d, fc), lambda i, j, te, nv: (layer, te[i], 0, chunk(i, j, nv))),
            pl.BlockSpec((None, None, d, fc), lambda i, j, te, nv: (layer, te[i], 0, chunk(i, j, nv) + n_fc)),
            pl.BlockSpec((None, None, fc, d), lambda i, j, te, nv: (layer, te[i], chunk(i, j, nv), 0)),
        ],
        out_specs=pl.BlockSpec(memory_space=pl.ANY),
        scratch_shapes=[
            pltpu.VMEM((n_fc, tm // n_fc, SUBLANES, LANES), F32),
            pltpu.VMEM((tm, d), BF16),
            pltpu.VMEM((2, n_fc, tm // n_fc, SUBLANES, LANES), F32),
            pltpu.SemaphoreType.DMA((1,)),
            pltpu.SemaphoreType.DMA((1,)),
        ],
    )
    return pl.pallas_call(
        functools.partial(_moe_kernel, n_fc=n_fc),
        grid_spec=grid_spec,
        out_shape=jax.ShapeDtypeStruct((2 * t + 2 * tm, SUBLANES, LANES), F32),
        compiler_params=_params(("arbitrary", "arbitrary")),
        name="moe",
    )(tile_expert, tile_nvalid, row_tok, row_tok, prev_dst, h, w13, w13, w2)


def _route_rows(e1, e2, tm, n_tiles):
    t = e1.shape[0]
    experts = jnp.concatenate([e1, e2])
    order = jnp.argsort(experts, stable=True).astype(jnp.int32)
    counts = jnp.sum(experts[:, None] == jnp.arange(N_EXPERTS)[None, :], axis=0).astype(jnp.int32)
    tiles_per = (counts + tm - 1) // tm
    tile_end = jnp.cumsum(tiles_per)
    tile_start = tile_end - tiles_per
    group_start = jnp.cumsum(counts) - counts
    tile_ids = jnp.arange(n_tiles, dtype=jnp.int32)
    te = jnp.minimum(jnp.sum(tile_ids[:, None] >= tile_end[None, :], axis=1), N_EXPERTS - 1).astype(jnp.int32)
    used = tile_ids < tile_end[-1]
    first_row = (tile_ids - tile_start[te]) * tm
    nvalid = jnp.where(used, jnp.clip(counts[te] - first_row, 0, tm), 0).astype(jnp.int32)
    last_e = te[jnp.maximum(tile_end[-1] - 1, 0)]
    te = jnp.where(used, te, last_e)
    r = jnp.arange(tm, dtype=jnp.int32)
    src = group_start[te][:, None] + first_row[:, None] + r[None, :]
    valid = r[None, :] < nvalid[:, None]
    dump = 2 * t + (tile_ids % 2)[:, None] * tm + r[None, :]
    row_dst = jnp.where(valid, order[jnp.clip(src, 0, 2 * t - 1)], dump).astype(jnp.int32)
    return te, nvalid, row_dst.reshape(n_tiles, 1, tm)


def _combine_kernel(x_ref, y0_ref, y1_ref, gw_ref, g2_ref, lng_ref, lnb_ref, o_ref):
    gw = gw_ref[...]
    rows = x_ref.shape
    f = gw[:, 2:3] * y0_ref[...].reshape(rows) + gw[:, 3:4] * y1_ref[...].reshape(rows)
    y = ALPHA * x_ref[...] + (1.0 + g2_ref[...]) * f
    o_ref[...] = _layer_norm(y, lng_ref[...], lnb_ref[...])


def _combine(x, y, gw, g2, lng, lnb, seq):
    t, d = x.shape
    tm = ROW_TILE
    per_seq = seq // tm
    n_t = t // tm
    vec = lambda: pl.BlockSpec((1, d), lambda i: (0, 0))
    return pl.pallas_call(
        _combine_kernel,
        grid=(n_t,),
        in_specs=[
            pl.BlockSpec((tm, d), lambda i: (i, 0)),
            pl.BlockSpec((tm, SUBLANES, LANES), lambda i: (i, 0, 0)),
            pl.BlockSpec((tm, SUBLANES, LANES), lambda i: (i + n_t, 0, 0)),
            pl.BlockSpec((tm, LANES), lambda i: (i, 0)),
            pl.BlockSpec((None, 1, d), lambda i: (i // per_seq, 0, 0)),
            vec(), vec(),
        ],
        out_specs=pl.BlockSpec((tm, d), lambda i: (i, 0)),
        out_shape=jax.ShapeDtypeStruct((t, d), F32),
        compiler_params=_params(("arbitrary",)),
        name="combine",
    )(x, y, y, gw, g2, lng, lnb)


def _kv_kernel(x_ref, sc_ref, sh_ref, w_ref, ka_ref, va_ref, km_ref, *, blocks_per_seq):
    h = _modulate(x_ref[...], sc_ref[...], sh_ref[...]).astype(BF16)
    kv = jnp.dot(h, w_ref[...], preferred_element_type=F32)
    k = kv[:, :D_MODEL]
    v = kv[:, D_MODEL:]
    km_ref[...] = jnp.mean(k, axis=0, keepdims=True)
    j = pl.program_id(0) % blocks_per_seq
    shape = (MOBA_BLOCK, HEAD_DIM)
    col = lax.broadcasted_iota(jnp.int32, shape, 1)
    pos = lax.broadcasted_iota(jnp.int32, shape, 0).astype(F32)
    blk = (j * MOBA_BLOCK).astype(F32)
    kx = jnp.where(col < 3, pos,
                   jnp.where(col < 6, blk,
                             jnp.where((col >= 8) & (col < 11), 1.0,
                                       jnp.where(col == 16 + j, 1.0, 0.0)))).astype(BF16)
    vx = jnp.where(col == 0, 1.0, 0.0).astype(BF16)
    for hd in range(N_HEADS):
        lo = hd * HEAD_DIM
        ka_ref[hd] = jnp.concatenate([k[:, lo:lo + HEAD_DIM].astype(BF16), kx], axis=1)
        va_ref[hd] = jnp.concatenate([v[:, lo:lo + HEAD_DIM].astype(BF16), vx], axis=1)


def _shared_kv(x, sc, sh, w, batch, seq):
    t, d = x.shape
    tm = MOBA_BLOCK
    n_blk = seq // tm
    mod = lambda: pl.BlockSpec((None, 1, d), lambda i: (i // n_blk, 0, 0))
    aug = lambda: pl.BlockSpec((None, N_HEADS, tm, AUG), lambda i: (i // n_blk, 0, i % n_blk, 0))
    return pl.pallas_call(
        functools.partial(_kv_kernel, blocks_per_seq=n_blk),
        grid=(t // tm,),
        in_specs=[
            pl.BlockSpec((tm, d), lambda i: (i, 0)),
            mod(), mod(),
            pl.BlockSpec((d, 2 * d), lambda i: (0, 0)),
        ],
        out_specs=[aug(), aug(), pl.BlockSpec((None, None, 1, d), lambda i: (i // n_blk, i % n_blk, 0, 0))],
        out_shape=[
            jax.ShapeDtypeStruct((batch, N_HEADS, seq, AUG), BF16),
            jax.ShapeDtypeStruct((batch, N_HEADS, seq, AUG), BF16),
            jax.ShapeDtypeStruct((batch, n_blk, 1, d), F32),
        ],
        compiler_params=_params(("arbitrary",)),
        name="shared_kv",
    )(x, sc, sh, w)


def _split3(x):
    p1 = x.astype(BF16).astype(F32)
    p2 = (x - p1).astype(BF16).astype(F32)
    p3 = (x - p1 - p2).astype(BF16).astype(F32)
    return p1, p2, p3


def _dot_nt(a, b):
    return lax.dot_general(a, b, (((1,), (1,)), ((), ())), preferred_element_type=F32)


def _qaug_kernel(sl_ref, x_ref, sc_ref, sh_ref, w_ref, kmbd_ref, qa_ref, *, n_blk):
    i = pl.program_id(0) % n_blk
    bq = MOBA_BLOCK
    h = _modulate(x_ref[...], sc_ref[...], sh_ref[...]).astype(BF16)
    qb = (jnp.dot(h, w_ref[...], preferred_element_type=F32) * (HEAD_DIM ** -0.5 * LOG2E)).astype(BF16)
    gates = _dot_nt(kmbd_ref[0], qb) + _dot_nt(kmbd_ref[1], qb) + _dot_nt(kmbd_ref[2], qb)
    lane_q = lax.broadcasted_iota(jnp.int32, (SUBLANES, bq), 1).astype(F32)
    sub8 = lax.broadcasted_iota(jnp.int32, (SUBLANES, bq), 0)
    blk_id = lax.broadcasted_iota(jnp.int32, (n_blk, bq), 0)
    t_q = lane_q + (i * bq).astype(F32)
    group = LANES // EXT
    for g0 in range(0, N_HEADS, group):
        exts = []
        for hd in range(g0, g0 + group):
            gate = jnp.where(blk_id < i, gates[hd * n_blk:(hd + 1) * n_blk, :], -jnp.inf)
            rank = jnp.zeros((n_blk, bq), jnp.int32)
            for m in range(n_blk):
                gm = gate[m:m + 1, :]
                ahead = (gm > gate) | ((gm == gate) & (blk_id > m))
                rank = rank + ahead.astype(jnp.int32)
            bias = jnp.where((blk_id < i) & (rank >= MOBA_TOPK), NEG_INF, 0.0)
            slope = jnp.full((SUBLANES, bq), sl_ref[hd], F32) * LOG2E
            s1, s2, s3 = _split3(slope)
            t1, t2, t3 = _split3(-slope * t_q)
            rows_s = jnp.where((sub8 == 0) | (sub8 == 3), s1,
                               jnp.where((sub8 == 1) | (sub8 == 4), s2,
                                         jnp.where((sub8 == 2) | (sub8 == 5), s3, 0.0)))
            rows_t = jnp.where(sub8 == 0, t1, jnp.where(sub8 == 1, t2, jnp.where(sub8 == 2, t3, 0.0)))
            pieces = [rows_s, rows_t, bias]
            if EXT - 2 * SUBLANES - n_blk:
                pieces.append(jnp.zeros((EXT - 2 * SUBLANES - n_blk, bq), F32))
            exts.extend(pieces)
        ext_t = jnp.concatenate(exts, axis=0).T.astype(BF16)
        pad = jnp.zeros((bq, AUG - HEAD_DIM - EXT), BF16)
        for k, hd in enumerate(range(g0, g0 + group)):
            qa_ref[hd] = jnp.concatenate([qb[:, hd * HEAD_DIM:(hd + 1) * HEAD_DIM],
                                          ext_t[:, k * EXT:(k + 1) * EXT], pad], axis=1)


def _q_aug(slopes, x, sc, sh, w, layer, kmbd, batch, seq):
    t, d = x.shape
    tm = MOBA_BLOCK
    n_blk = seq // tm
    rows = kmbd.shape[2]
    mod = lambda: pl.BlockSpec((None, 1, d), lambda i, sl: (i // n_blk, 0, 0))
    grid_spec = pltpu.PrefetchScalarGridSpec(
        num_scalar_prefetch=1,
        grid=(t // tm,),
        in_specs=[
            pl.BlockSpec((tm, d), lambda i, sl: (i, 0)),
            mod(), mod(),
            pl.BlockSpec((None, d, d), lambda i, sl: (layer, 0, 0)),
            pl.BlockSpec((None, 3, rows, d), lambda i, sl: (i // n_blk, 0, 0, 0)),
        ],
        out_specs=pl.BlockSpec((None, N_HEADS, tm, AUG), lambda i, sl: (i // n_blk, 0, i % n_blk, 0)),
    )
    return pl.pallas_call(
        functools.partial(_qaug_kernel, n_blk=n_blk),
        grid_spec=grid_spec,
        out_shape=jax.ShapeDtypeStruct((batch, N_HEADS, seq, AUG), BF16),
        compiler_params=_params(("arbitrary",)),
        name="q_aug",
    )(slopes, x, sc, sh, w, kmbd)


def _block_diag_means(km, n_blk):
    d = km.shape[-1]
    r = jnp.arange(N_HEADS * n_blk)
    head_of_col = jnp.arange(d) // HEAD_DIM
    bd = jnp.where((r // n_blk)[None, :, None] == head_of_col[None, None, :], km[:, r % n_blk, :], 0.0)
    return jnp.stack(_split3(bd), axis=1).astype(BF16)


def _attn_kernel(qlo_ref, qhi_ref, ka_ref, va_ref, olo_ref, ohi_ref, q_scr, s_scr, *, n_blk):
    i = pl.program_id(2)
    bq = MOBA_BLOCK
    half = n_blk // 2
    q_scr[0] = qlo_ref[...]
    q_scr[1] = qhi_ref[...]
    dmat = lax.broadcasted_iota(jnp.int32, (bq, bq), 1) - lax.broadcasted_iota(jnp.int32, (bq, bq), 0)

    def unit(t):
        if t >= half:
            return None, 1, pl.multiple_of((t - i - 1) * bq, bq)
        is_lo = i >= t
        key = jnp.where(is_lo, t, t - i - 1)
        return is_lo, jnp.where(is_lo, 0, 1), pl.multiple_of(key * bq, bq)

    neg = jnp.full((bq, LANES), NEG_INF, F32)
    zero = jnp.zeros((bq, AUG), F32)
    mx = [[neg] * HEAD_PAIR, [neg] * HEAD_PAIR]
    acc = [[zero] * HEAD_PAIR, [zero] * HEAD_PAIR]
    m_rows = [[None] * HEAD_PAIR, [None] * HEAD_PAIR]

    def score_unit(t, hh):
        is_lo, sel, start = unit(t)
        s = _dot_nt(q_scr[sel, hh], ka_ref[hh, pl.ds(start, bq), :])
        if is_lo is not None:
            s = jnp.where(dmat <= jnp.where(is_lo, i - t, n_blk - t) * bq, s, NEG_INF)
        elif t == n_blk:
            s = jnp.where(dmat <= 0, s, NEG_INF)
        s_scr[hh, t] = s
        smax = jnp.maximum(s[:, :LANES], s[:, LANES:])
        if is_lo is None:
            mx[1][hh] = jnp.maximum(mx[1][hh], smax)
        else:
            mx[0][hh] = jnp.maximum(mx[0][hh], jnp.where(is_lo, smax, NEG_INF))
            mx[1][hh] = jnp.maximum(mx[1][hh], jnp.where(is_lo, NEG_INF, smax))

    def finish_max(hh):
        for sel in range(2):
            m_rows[sel][hh] = jnp.broadcast_to(jnp.max(mx[sel][hh], axis=1, keepdims=True), (bq, LANES))

    def value_unit(t, hh):
        is_lo, _, start = unit(t)
        m_row = m_rows[1][hh] if is_lo is None else jnp.where(is_lo, m_rows[0][hh], m_rows[1][hh])
        p = jnp.exp2(s_scr[hh, t] - jnp.concatenate([m_row, m_row], axis=1)).astype(BF16)
        pv = jnp.dot(p, va_ref[hh, pl.ds(start, bq), :], preferred_element_type=F32)
        if is_lo is None:
            acc[1][hh] = acc[1][hh] + pv
        else:
            acc[0][hh] = acc[0][hh] + jnp.where(is_lo, pv, 0.0)
            acc[1][hh] = acc[1][hh] + jnp.where(is_lo, 0.0, pv)

    units = range(n_blk + 1)
    for t in units:
        score_unit(t, 0)
    for hh in range(HEAD_PAIR):
        finish_max(hh)
        for t in units:
            if hh + 1 < HEAD_PAIR:
                score_unit(t, hh + 1)
            value_unit(t, hh)

    for sel, o_ref in enumerate((olo_ref, ohi_ref)):
        outs = [a[:, :HEAD_DIM] / a[:, HEAD_DIM:HEAD_DIM + 1] for a in acc[sel]]
        o_ref[...] = jnp.concatenate(outs, axis=1).astype(o_ref.dtype)


def _attention(qa, ka, va, batch, seq):
    n_blk = seq // MOBA_BLOCK
    assert n_blk % 2 == 0
    pair = lambda: pl.BlockSpec((None, HEAD_PAIR, seq, AUG), lambda b, hp, i: (b, hp, 0, 0))
    half = n_blk // 2
    o_shape = jax.ShapeDtypeStruct((batch, seq // 2, D_MODEL), BF16)
    out_lo, out_hi = pl.pallas_call(
        functools.partial(_attn_kernel, n_blk=n_blk),
        grid=(batch, N_HEADS // HEAD_PAIR, n_blk // 2),
        in_specs=[
            pl.BlockSpec((None, HEAD_PAIR, MOBA_BLOCK, AUG), lambda b, hp, i: (b, hp, i, 0)),
            pl.BlockSpec((None, HEAD_PAIR, MOBA_BLOCK, AUG), lambda b, hp, i: (b, hp, n_blk - 1 - i, 0)),
            pair(), pair(),
        ],
        out_specs=[
            pl.BlockSpec((None, MOBA_BLOCK, HEAD_PAIR * HEAD_DIM), lambda b, hp, i: (b, i, hp)),
            pl.BlockSpec((None, MOBA_BLOCK, HEAD_PAIR * HEAD_DIM), lambda b, hp, i: (b, half - 1 - i, hp)),
        ],
        out_shape=[o_shape, o_shape],
        scratch_shapes=[
            pltpu.VMEM((2, HEAD_PAIR, MOBA_BLOCK, AUG), BF16),
            pltpu.VMEM((HEAD_PAIR, n_blk + 1, MOBA_BLOCK, MOBA_BLOCK), F32),
        ],
        compiler_params=_params(("arbitrary", "arbitrary", "arbitrary")),
        name="moba_attn",
    )(qa, qa, ka, va)
    return out_lo, out_hi


def kernel(x, c, ada_w, ada_b, ln_g, ln_b, conv_in_w, conv_in_b, conv_dw_w, conv_dw_b, conv_norm_g,
           conv_norm_b, conv_out_w, conv_out_b, kv_ada_w, kv_ada_b, w_kv, w_q, w_o, ffn_w13, ffn_w2,
           router_w, router_b, moe_w13, moe_w2):
    batch, seq, d = x.shape
    t = batch * seq
    assert d == D_MODEL and seq % ROW_TILE == 0 and seq % MOBA_BLOCK == 0 and t & (t - 1) == 0
    n_blk = seq // MOBA_BLOCK
    assert n_blk <= 16

    c_pad = jnp.pad(c, ((0, SUBLANES - batch), (0, 0)))
    mods = _ada(c_pad, ada_w, ada_b, 1536)[:, :batch]
    kv_mod = _ada(c_pad, kv_ada_w[None], kv_ada_b[None], 1024)[0, :batch]

    def mod_vec(v):
        return v.reshape(batch, 1, d)

    row = lambda v: v.reshape(1, -1)
    slopes = jnp.exp2(-8.0 * jnp.arange(1, N_HEADS + 1, dtype=F32) / N_HEADS)
    n_moe_tiles = 2 * t // MOE_TILE + N_EXPERTS
    conv_in_wb, conv_out_wb, w_qb, w_ob = [w.astype(BF16) for w in (conv_in_w, conv_out_w, w_q, w_o)]
    ffn_w13b, ffn_w2b, moe_w13b, moe_w2b = [w.astype(BF16) for w in (ffn_w13, ffn_w2, moe_w13, moe_w2)]

    xf = x.reshape(t, d)
    ka = va = kmbd = None
    for l in range(DEPTH):
        if l == N_A_LAYERS:
            ka, va, km = _shared_kv(xf, mod_vec(kv_mod[:, d:]), mod_vec(kv_mod[:, :d]),
                                    w_kv.astype(BF16), batch, seq)
            kmbd = _block_diag_means(km.reshape(batch, n_blk, d), n_blk)
        sh1, sc1, g1, sh2, sc2, g2 = [mod_vec(mods[l, :, k * d:(k + 1) * d]) for k in range(6)]
        lng1, lnb1 = row(ln_g[l, 0]), row(ln_b[l, 0])
        lng2, lnb2 = row(ln_g[l, 1]), row(ln_b[l, 1])
        if l < N_A_LAYERS:
            u = _conv_in(xf, sc1, sh1, conv_in_wb, l, row(conv_in_b[l]), seq)
            xf = _conv_block(u, xf, g1, conv_dw_w[l], row(conv_dw_b[l]), row(conv_norm_g[l]),
                             row(conv_norm_b[l]), conv_out_wb, l, row(conv_out_b[l]), lng1, lnb1, seq)
        else:
            jl = l - N_A_LAYERS
            qa = _q_aug(slopes, xf, sc1, sh1, w_qb, jl, kmbd, batch, seq)
            att_lo, att_hi = _attention(qa, ka, va, batch, seq)
            xf = _oproj(att_lo, att_hi, xf, g1, w_ob, jl, lng1, lnb1, seq)
        if l % 2 == 0:
            xf = _ffn(xf, sc2, sh2, g2, ffn_w13b, ffn_w2b, l // 2, lng2, lnb2, seq)
        else:
            e = l // 2
            h, meta_rows, meta_cols = _router(xf, sc2, sh2, router_w[e].T, router_b[e].reshape(-1, 1), seq)
            te, nvalid, row_dst = _route_rows(meta_rows[0].astype(jnp.int32), meta_rows[1].astype(jnp.int32),
                                              MOE_TILE, n_moe_tiles)
            y = _moe(h, te, nvalid, row_dst, moe_w13b, moe_w2b, e)
            xf = _combine(xf, y, meta_cols, g2, lng2, lnb2, seq)
    return xf.reshape(batch, seq, d)
```
